```python
import math
import jax
import jax.numpy as jnp
from jax import lax
import numpy as np

D_MODEL = 2048
BATCH = 4
SEQ = 2048
DEPTH = 4
DEC_BATCH = 8
DEC_SEQ = 4
PAST_LEN = 16384
PAGE_SIZE = 128

HEAD_DIM = 128
MIX_WIDTH = D_MODEL
H_A = MIX_WIDTH // (4 * HEAD_DIM)
D_A = H_A * HEAD_DIM
CHUNK_A = 128
H_B = (MIX_WIDTH - D_A) // (2 * HEAD_DIM)
D_B = H_B * HEAD_DIM
H_C = (MIX_WIDTH - D_A - D_B) // HEAD_DIM
D_C = H_C * HEAD_DIM
DC_HALF = HEAD_DIM // 2
CONV_B = 4
CHUNK_B = 64
D_FF = 256 * ((8 * D_MODEL // 3 + 255) // 256)
CONV_FFN = 3
Q_BLOCK = 128
EPS = 1e-6

OFF_AU = 0
OFF_AV = OFF_AU + D_A
OFF_BQKV = OFF_AV + D_A
OFF_BZ = OFF_BQKV + 3 * D_B
OFF_BA = OFF_BZ + D_B
OFF_BB = OFF_BA + H_B
OFF_CQ = OFF_BB + H_B
OFF_CK = OFF_CQ + D_C
OFF_CV = OFF_CK + D_C
N_IN = OFF_CV + D_C

kernel_name = "hybrid_chunkmlp_gdn_diffattn_step"


def rmsnorm(x, g):
    xf = x.astype(jnp.float32)
    y = xf * lax.rsqrt(jnp.mean(xf * xf, axis=-1, keepdims=True) + EPS)
    return (y * g.astype(jnp.float32)).astype(x.dtype)


def l2norm(x):
    return x * lax.rsqrt(jnp.sum(x * x, axis=-1, keepdims=True) + EPS)


def causal_dwconv(x, hist, w, b=None):
    width = w.shape[0]
    T = x.shape[1]
    xp = jnp.concatenate([hist.astype(x.dtype), x], axis=1)
    y = xp[:, 0:T] * w[0]
    for i in range(1, width):
        y = y + xp[:, i:i + T] * w[i]
    if b is not None:
        y = y + b
    return y, xp[:, T:]


def chunk_mlp(u, v, ws, bs):
    B, T = v.shape[:2]
    pad = (-T) % CHUNK_A
    n = (T + pad) // CHUNK_A
    vc = jnp.pad(v, ((0, 0), (0, pad), (0, 0), (0, 0))).reshape(B, n, CHUNK_A, H_A, HEAD_DIM)
    mask = jnp.tril(jnp.ones((CHUNK_A, CHUNK_A), dtype=bool))
    wm = jnp.where(mask, ws, jnp.zeros_like(ws))
    mixed = jnp.einsum('hts,bnshc->bnthc', wm, vc) + bs.T[:, :, None]
    mixed = mixed.reshape(B, n * CHUNK_A, H_A, HEAD_DIM)[:, :T]
    return u * mixed


def gated_delta_chunked(q, k, v, g, beta, S0):
    B, T, H, Dk = q.shape
    Dv = v.shape[-1]
    pad = (-T) % CHUNK_B
    n = (T + pad) // CHUNK_B

    def prep(a):
        a = jnp.pad(a, [(0, 0), (0, pad)] + [(0, 0)] * (a.ndim - 2))
        a = a.reshape((B, n, CHUNK_B) + a.shape[2:])
        return jnp.moveaxis(a, 3, 1)

    q, k, v, g, beta = prep(q), prep(k), prep(v), prep(g), prep(beta)
    gc = jnp.cumsum(g, axis=-1)
    idx = jnp.arange(CHUNK_B)
    incl = idx[:, None] >= idx[None, :]
    strict = idx[:, None] > idx[None, :]
    diff = gc[..., :, None] - gc[..., None, :]
    decay = jnp.exp(jnp.where(incl, diff, -jnp.inf))
    kb = k * beta[..., None]
    vb = v * beta[..., None]
    A = jnp.einsum('bhnid,bhnjd->bhnij', kb, k) * jnp.where(strict, decay, 0.0)
    rhs = jnp.concatenate([vb, kb * jnp.exp(gc)[..., None]], axis=-1)
    sol = lax.linalg.triangular_solve(A, rhs, left_side=True, lower=True, unit_diagonal=True)
    uu, ww = sol[..., :Dv], sol[..., Dv:]
    qk = jnp.einsum('bhnid,bhnjd->bhnij', q, k) * decay
    g_last = gc[..., -1]

    def step(S, xs):
        q_i, k_i, u_i, w_i, qk_i, gc_i, gl_i = xs
        v_new = u_i - jnp.einsum('bhck,bhkv->bhcv', w_i, S)
        o = (jnp.einsum('bhck,bhkv->bhcv', q_i * jnp.exp(gc_i)[..., None], S)
             + jnp.einsum('bhij,bhjv->bhiv', qk_i, v_new))
        S = (S * jnp.exp(gl_i)[..., None, None]
             + jnp.einsum('bhck,bhcv->bhkv', k_i * jnp.exp(gl_i[..., None] - gc_i)[..., None], v_new))
        return S, o

    xs = tuple(jnp.moveaxis(a, 2, 0) for a in (q, k, uu, ww, qk, gc, g_last))
    S, o = lax.scan(step, S0, xs)
    o = jnp.moveaxis(o, 0, 2).reshape(B, H, n * CHUNK_B, Dv).transpose(0, 2, 1, 3)[:, :T]
    return o, S


def delta_mixer(qkv_raw, z, a, b, conv_hist, S0, conv_w, a_log, dt_bias, norm_g):
    B, T, _ = qkv_raw.shape
    f32 = jnp.float32
    qkv, conv_new = causal_dwconv(qkv_raw, conv_hist, conv_w)
    qkv = jax.nn.silu(qkv.astype(f32)).reshape(B, T, 3, H_B, HEAD_DIM)
    q = l2norm(qkv[:, :, 0]) * (HEAD_DIM ** -0.5)
    k = l2norm(qkv[:, :, 1])
    v = qkv[:, :, 2]
    g = -jnp.exp(a_log.astype(f32)) * jax.nn.softplus(a.astype(f32) + dt_bias.astype(f32))
    beta = jax.nn.sigmoid(b.astype(f32))
    o, S = gated_delta_chunked(q, k, v, g, beta, S0)
    o = rmsnorm(o, norm_g) * jax.nn.silu(z.astype(f32).reshape(B, T, H_B, HEAD_DIM))
    return o.reshape(B, T, D_B).astype(qkv_raw.dtype), conv_new, S


def diff_attention(q, k, v, lam):
    B, T, H, _ = q.shape
    S = k.shape[1]
    q_start = S - T
    qb = min(Q_BLOCK, T)
    pad = (-T) % qb
    n = (T + pad) // qb
    qp = jnp.pad(q, ((0, 0), (0, pad), (0, 0), (0, 0))).reshape(B, n, qb, H, 2, DC_HALF)
    qp = jnp.moveaxis(qp, 1, 0)
    qpos = (q_start + jnp.arange(T + pad)).reshape(n, qb)
    kpos = jnp.arange(S)
    kk = k.reshape(B, S, H, 2, DC_HALF)
    scale = DC_HALF ** -0.5

    def block(args):
        qblk, pblk = args
        s = jnp.einsum('bqhmd,bkhmd->bhmqk', qblk, kk).astype(jnp.float32) * scale
        mask = kpos[None, :] <= pblk[:, None]
        p = jax.nn.softmax(jnp.where(mask, s, -jnp.inf), axis=-1)
        w = p[:, :, 0] - lam * p[:, :, 1]
        return jnp.einsum('bhqk,bkhd->bqhd', w, v)

    o = lax.map(block, (qp, qpos))
    return jnp.moveaxis(o, 0, 1).reshape(B, n * qb, H, HEAD_DIM)[:, :T]


def conv_ffn(h, hist, w_up, cw, cb, w_down):
    up = h @ w_up
    up, hist_new = causal_dwconv(up, hist, cw, cb)
    gate, val = jnp.split(up, 2, axis=-1)
    return (jax.nn.silu(gate) * val) @ w_down, hist_new


def run_trunk(x, cache_k, cache_v, page_table, state_delta, state_conv_qkv, state_ffn_conv,
              norm1_g, w_in, a_norm_g, a_ws, a_bs, b_conv_w, b_a_log, b_dt_bias, b_norm_g,
              c_lam_q1, c_lam_k1, c_lam_q2, c_lam_k2, c_norm_g, w_out, norm2_g, w_up,
              ffn_conv_w, ffn_conv_b, w_down, final_g):
    f32 = jnp.float32
    B, T, _ = x.shape
    has_past = cache_k is not None
    k_rows, v_rows, deltas, convs, ffns, chunk_vs = [], [], [], [], [], []
    for l in range(DEPTH):
        h = rmsnorm(x, norm1_g[l])
        p = h @ w_in[l]
        u = jax.nn.gelu(p[..., OFF_AU:OFF_AV]).reshape(B, T, H_A, HEAD_DIM)
        va = rmsnorm(jax.nn.gelu(p[..., OFF_AV:OFF_BQKV]).reshape(B, T, H_A, HEAD_DIM), a_norm_g[l])
        out_a = chunk_mlp(u, va, a_ws[l], a_bs[l]).reshape(B, T, D_A)
        if has_past:
            conv_hist = state_conv_qkv[:, l]
            S0 = state_delta[:, l].astype(f32)
        else:
            conv_hist = jnp.zeros((B, CONV_B - 1, 3 * D_B), x.dtype)
            S0 = jnp.zeros((B, H_B, HEAD_DIM, HEAD_DIM), f32)
        out_b, conv_new, S_new = delta_mixer(
            p[..., OFF_BQKV:OFF_BZ], p[..., OFF_BZ:OFF_BA], p[..., OFF_BA:OFF_BB], p[..., OFF_BB:OFF_CQ],
            conv_hist, S0, b_conv_w[l], b_a_log[l], b_dt_bias[l], b_norm_g[l])
        qc = p[..., OFF_CQ:OFF_CK].reshape(B, T, H_C, HEAD_DIM)
        kc = p[..., OFF_CK:OFF_CV].reshape(B, T, H_C, HEAD_DIM)
        vc = p[..., OFF_CV:N_IN].reshape(B, T, H_C, HEAD_DIM)
        if has_past:
            past_k = cache_k[page_table, l].reshape(B, -1, H_C, HEAD_DIM).astype(kc.dtype)
            past_v = cache_v[page_table, l].reshape(B, -1, H_C, HEAD_DIM).astype(vc.dtype)
            keys = jnp.concatenate([past_k, kc], axis=1)
            vals = jnp.concatenate([past_v, vc], axis=1)
        else:
            keys, vals = kc, vc
        lam_init = 0.8 - 0.6 * math.exp(-0.3 * l)
        lam = (jnp.exp(jnp.sum(c_lam_q1[l].astype(f32) * c_lam_k1[l].astype(f32)))
               - jnp.exp(jnp.sum(c_lam_q2[l].astype(f32) * c_lam_k2[l].astype(f32))) + lam_init)
        oc = diff_attention(qc, keys, vals, lam)
        oc = rmsnorm(oc, c_norm_g[l]) * (1.0 - lam_init)
        mix = jnp.concatenate([out_a.astype(x.dtype), out_b, oc.reshape(B, T, D_C).astype(x.dtype)], axis=-1)
        x = x + mix @ w_out[l]
        fh = state_ffn_conv[:, l] if has_past else jnp.zeros((B, CONV_FFN - 1, 2 * D_FF), x.dtype)
        f, fh_new = conv_ffn(rmsnorm(x, norm2_g[l]), fh, w_up[l], ffn_conv_w[l], ffn_conv_b[l], w_down[l])
        x = x + f
        k_rows.append(kc)
        v_rows.append(vc)
        deltas.append(S_new)
        convs.append(conv_new)
        ffns.append(fh_new)
        if has_past:
            chunk_vs.append(va)
    y = rmsnorm(x, final_g)
    chunk_v = jnp.stack(chunk_vs, axis=1) if has_past else None
    return (y, jnp.stack(k_rows, axis=1), jnp.stack(v_rows, axis=1), jnp.stack(deltas, axis=1),
            jnp.stack(convs, axis=1), jnp.stack(ffns, axis=1), chunk_v)


def setup_inputs(seed: int = 0) -> dict:
    key = jax.random.key(seed)
    ks = jax.random.split(key, 40)
    f32 = jnp.float32

    def nrm(i, shape, scale):
        return jax.random.normal(ks[i], shape, f32) * scale

    def gain(i, shape):
        return 1.0 + nrm(i, shape, 0.02)

    n_pages = PAST_LEN // PAGE_SIZE
    n_used = DEC_BATCH * n_pages
    n_pool = n_used + (n_used + 3) // 4
    page_table = jax.random.permutation(ks[0], n_pool)[:n_used].reshape(DEC_BATCH, n_pages).astype(jnp.int32)
    dt = jnp.exp(jax.random.uniform(ks[1], (DEPTH, H_B), f32, math.log(1e-3), math.log(1e-1)))
    return {
        "x_prompt": nrm(2, (BATCH, SEQ, D_MODEL), 1.0),
        "x_sample": nrm(3, (DEC_BATCH, DEC_SEQ, D_MODEL), 1.0),
        "cache_k": nrm(4, (n_pool, DEPTH, PAGE_SIZE, H_C, HEAD_DIM), 1.0),
        "cache_v": nrm(5, (n_pool, DEPTH, PAGE_SIZE, H_C, HEAD_DIM), 1.0),
        "page_table": page_table,
        "state_delta": nrm(6, (DEC_BATCH, DEPTH, H_B, HEAD_DIM, HEAD_DIM), 0.1),
        "state_conv_qkv": nrm(7, (DEC_BATCH, DEPTH, CONV_B - 1, 3 * D_B), 1.0),
        "state_ffn_conv": nrm(8, (DEC_BATCH, DEPTH, CONV_FFN - 1, 2 * D_FF), 1.0),
        "norm1_g": gain(9, (DEPTH, D_MODEL)),
        "w_in": nrm(10, (DEPTH, D_MODEL, N_IN), D_MODEL ** -0.5),
        "a_norm_g": gain(11, (DEPTH, H_A, HEAD_DIM)),
        "a_ws": nrm(12, (DEPTH, H_A, CHUNK_A, CHUNK_A), CHUNK_A ** -0.5),
        "a_bs": gain(13, (DEPTH, H_A, CHUNK_A)),
        "b_conv_w": nrm(14, (DEPTH, CONV_B, 3 * D_B), CONV_B ** -0.5),
        "b_a_log": jnp.log(jax.random.uniform(ks[15], (DEPTH, H_B), f32, 1.0, 16.0)),
        "b_dt_bias": dt + jnp.log(-jnp.expm1(-dt)),
        "b_norm_g": gain(16, (DEPTH, HEAD_DIM)),
        "c_lam_q1": nrm(17, (DEPTH, DC_HALF), 0.1),
        "c_lam_k1": nrm(18, (DEPTH, DC_HALF), 0.1),
        "c_lam_q2": nrm(19, (DEPTH, DC_HALF), 0.1),
        "c_lam_k2": nrm(20, (DEPTH, DC_HALF), 0.1),
        "c_norm_g": gain(21, (DEPTH, HEAD_DIM)),
        "w_out": nrm(22, (DEPTH, MIX_WIDTH, D_MODEL), MIX_WIDTH ** -0.5),
        "norm2_g": gain(23, (DEPTH, D_MODEL)),
        "w_up": nrm(24, (DEPTH, D_MODEL, 2 * D_FF), D_MODEL ** -0.5),
        "ffn_conv_w": nrm(25, (DEPTH, CONV_FFN, 2 * D_FF), CONV_FFN ** -0.5),
        "ffn_conv_b": nrm(26, (DEPTH, 2 * D_FF), 0.02),
        "w_down": nrm(27, (DEPTH, D_FF, D_MODEL), D_FF ** -0.5),
        "final_g": gain(28, (D_MODEL,)),
    }


def reference(x_prompt, x_sample, cache_k, cache_v, page_table, state_delta, state_conv_qkv,
              state_ffn_conv, norm1_g, w_in, a_norm_g, a_ws, a_bs, b_conv_w, b_a_log, b_dt_bias,
              b_norm_g, c_lam_q1, c_lam_k1, c_lam_q2, c_lam_k2, c_norm_g, w_out, norm2_g, w_up,
              ffn_conv_w, ffn_conv_b, w_down, final_g):
    y_prompt, prompt_k, prompt_v, prompt_delta, prompt_conv_qkv, prompt_ffn_conv, _none = run_trunk(
        x_prompt, None, None, None, None, None, None,
        norm1_g, w_in, a_norm_g, a_ws, a_bs, b_conv_w, b_a_log, b_dt_bias, b_norm_g,
        c_lam_q1, c_lam_k1, c_lam_q2, c_lam_k2, c_norm_g, w_out, norm2_g, w_up,
        ffn_conv_w, ffn_conv_b, w_down, final_g)
    y_sample, sample_k, sample_v, sample_delta, sample_conv_qkv, sample_ffn_conv, sample_chunk_v = run_trunk(
        x_sample, cache_k, cache_v, page_table, state_delta, state_conv_qkv, state_ffn_conv,
        norm1_g, w_in, a_norm_g, a_ws, a_bs, b_conv_w, b_a_log, b_dt_bias, b_norm_g,
        c_lam_q1, c_lam_k1, c_lam_q2, c_lam_k2, c_norm_g, w_out, norm2_g, w_up,
        ffn_conv_w, ffn_conv_b, w_down, final_g)
    return (y_prompt, y_sample, prompt_k, prompt_v, prompt_delta, prompt_conv_qkv, prompt_ffn_conv,
            sample_k, sample_v, sample_delta, sample_conv_qkv, sample_ffn_conv, sample_chunk_v)
```

```python
import functools
import math

import jax
import jax.numpy as jnp
from jax import lax
from jax.experimental import pallas as pl
from jax.experimental.pallas import tpu as pltpu

f32 = jnp.float32
bf16 = jnp.bfloat16

D_MODEL = 2048
DEPTH = 4
HEAD_DIM = 128
H_A = 4
D_A = H_A * HEAD_DIM
CHUNK_A = 128
H_B = 6
D_B = H_B * HEAD_DIM
H_C = 6
D_C = H_C * HEAD_DIM
DC_HALF = HEAD_DIM // 2
CONV_B = 4
CHUNK_B = 64
D_FF = 5632
CONV_FFN = 3
EPS = 1e-6
PAGE_SIZE = 128
SAMPLE_ROWS = 8

OFF_AU = 0
OFF_AV = OFF_AU + D_A
OFF_BQKV = OFF_AV + D_A
OFF_BZ = OFF_BQKV + 3 * D_B
OFF_BA = OFF_BZ + D_B
OFF_BB = OFF_BA + H_B
OFF_CQ = OFF_BB + H_B
OFF_CK = OFF_CQ + D_C
OFF_CV = OFF_CK + D_C
N_IN = OFF_CV + D_C

P_BQKV = 0
P_BZ = P_BQKV + 3 * D_B
P_CQ = P_BZ + D_B
P_CK = P_CQ + D_C
P_CV = P_CK + D_C
P_AU = P_CV + D_C
P_AV = P_AU + D_A
N_MAIN = P_AV + D_A
LANES = 128

VMEM_LIMIT = 56 * 1024 * 1024
NEG_BIG = -1e30
HI = lax.Precision.HIGHEST
NT = (((1,), (1,)), ((), ()))
TN = (((0,), (0,)), ((), ()))


def _params(*sem):
    return pltpu.CompilerParams(dimension_semantics=sem, vmem_limit_bytes=VMEM_LIMIT)


def _sigmoid(x):
    return 1.0 / (1.0 + jnp.exp(-x))


def _gelu_tanh(x):
    c = math.sqrt(2.0 / math.pi)
    return 0.5 * x * (1.0 + jnp.tanh(c * (x + 0.044715 * (x * x * x))))


def _rms(x, g):
    return x * lax.rsqrt(jnp.mean(x * x, axis=-1, keepdims=True) + EPS) * g


def _in_proj_kernel(x_ref, g_ref, w_ref, wab_ref, p_ref, pab_ref, h_scr):
    @pl.when(pl.program_id(1) == 0)
    def _():
        h = _rms(x_ref[...], g_ref[...]).astype(bf16)
        h_scr[...] = h
        pab_ref[...] = jnp.dot(h, wab_ref[...], preferred_element_type=f32)

    p_ref[...] = jnp.dot(h_scr[...], w_ref[...], preferred_element_type=f32)


def _in_proj(x, g, w, wab, tm, tn):
    m = x.shape[0]
    n = w.shape[1]
    return pl.pallas_call(
        _in_proj_kernel,
        grid=(m // tm, n // tn),
        in_specs=[
            pl.BlockSpec((tm, D_MODEL), lambda i, j: (i, 0)),
            pl.BlockSpec((1, D_MODEL), lambda i, j: (0, 0)),
            pl.BlockSpec((D_MODEL, tn), lambda i, j: (0, j)),
            pl.BlockSpec((D_MODEL, LANES), lambda i, j: (0, 0)),
        ],
        out_specs=[
            pl.BlockSpec((tm, tn), lambda i, j: (i, j)),
            pl.BlockSpec((tm, LANES), lambda i, j: (i, 0)),
        ],
        out_shape=[jax.ShapeDtypeStruct((m, n), f32), jax.ShapeDtypeStruct((m, LANES), f32)],
        scratch_shapes=[pltpu.VMEM((tm, D_MODEL), bf16)],
        compiler_params=_params("parallel", "arbitrary"),
        name="in_proj",
    )(x, g, w, wab)


def _chunk_mlp_kernel(u0_ref, u1_ref, v0_ref, v1_ref, g_ref, w_ref, bs_ref, oa_ref, va_ref, *, r, seq_shift):
    nsub = u0_ref.shape[0] // r
    row = lax.broadcasted_iota(jnp.int32, (r, r), 0)
    col = lax.broadcasted_iota(jnp.int32, (r, r), 1)
    same_seq = lax.shift_right_logical(row, seq_shift) == lax.shift_right_logical(col, seq_shift)
    mask = jnp.logical_and(same_seq, col <= row)
    g = g_ref[...]
    for h in range(H_A):
        wm = jnp.where(mask, w_ref[h], 0.0).astype(bf16)
        u_ref = (u0_ref, u1_ref)[h // 2]
        v_ref = (v0_ref, v1_ref)[h // 2]
        lo = (h % 2) * HEAD_DIM
        for c in range(nsub):
            rows = slice(c * r, (c + 1) * r)
            u = _gelu_tanh(u_ref[rows, lo:lo + HEAD_DIM])
            v = _rms(_gelu_tanh(v_ref[rows, lo:lo + HEAD_DIM]), g[:, h * HEAD_DIM:(h + 1) * HEAD_DIM])
            va_ref[rows, h * HEAD_DIM:(h + 1) * HEAD_DIM] = v
            mixed = jnp.dot(wm, v.astype(bf16), preferred_element_type=f32) + bs_ref[:, h:h + 1]
            oa_ref[rows, h * HEAD_DIM:(h + 1) * HEAD_DIM] = (u * mixed).astype(oa_ref.dtype)


def _chunk_mlp(p, g, w, bs_rows, r, seq_rows, rb, out_dtype):
    m = p.shape[0]
    half = 2 * HEAD_DIM
    cu, cv = P_AU // half, P_AV // half
    kern = functools.partial(_chunk_mlp_kernel, r=r, seq_shift=int(math.log2(seq_rows)))
    return pl.pallas_call(
        kern,
        grid=(m // rb,),
        in_specs=[
            pl.BlockSpec((rb, half), lambda i: (i, cu)),
            pl.BlockSpec((rb, half), lambda i: (i, cu + 1)),
            pl.BlockSpec((rb, half), lambda i: (i, cv)),
            pl.BlockSpec((rb, half), lambda i: (i, cv + 1)),
            pl.BlockSpec((1, D_A), lambda i: (0, 0)),
            pl.BlockSpec((H_A, r, r), lambda i: (0, 0, 0)),
            pl.BlockSpec((r, LANES), lambda i: (0, 0)),
        ],
        out_specs=[
            pl.BlockSpec((rb, D_A), lambda i: (i, 0)),
            pl.BlockSpec((rb, D_A), lambda i: (i, 0)),
        ],
        out_shape=[jax.ShapeDtypeStruct((m, D_A), out_dtype), jax.ShapeDtypeStruct((m, D_A), f32)],
        compiler_params=_params("parallel"),
        name="chunk_mlp",
    )(p, p, p, p, g, w, bs_rows)


def _softplus(x):
    return jnp.maximum(x, 0.0) + jnp.log(1.0 + jnp.exp(-jnp.abs(x)))


def _delta_kernel(x_ref, z_ref, ab_ref, hist_ref, s0_ref, cw_ref, alog_ref, dtb_ref, ng_ref,
                  o_ref, s_ref, cn_ref, ext, *, rows, t_valid, nchunks):
    c = CHUNK_B
    ci = pl.program_id(1)

    @pl.when(ci == 0)
    def _():
        ext[0:8, :] = jnp.zeros((8, 3 * D_B), f32)
        ext[5:8, :] = hist_ref[...]
        s_ref[...] = s0_ref[...]

    x = x_ref[...]
    ext[8:8 + rows, :] = x
    cw = cw_ref[...]
    y = (ext[5:5 + rows, :] * cw[0:1] + ext[6:6 + rows, :] * cw[1:2]
         + ext[7:7 + rows, :] * cw[2:3] + x * cw[3:4])

    @pl.when(ci == nchunks - 1)
    def _():
        cn_ref[...] = ext[5 + t_valid:8 + t_valid, :]

    ext[0:8, :] = ext[rows:rows + 8, :]

    y = y * _sigmoid(y)
    z = z_ref[...]
    ab = ab_ref[...]
    if rows < c:
        y = jnp.concatenate([y, jnp.zeros((c - rows, y.shape[1]), f32)], axis=0)
        z = jnp.concatenate([z, jnp.zeros((c - rows, z.shape[1]), f32)], axis=0)
        ab = jnp.concatenate([ab, jnp.zeros((c - rows, ab.shape[1]), f32)], axis=0)
    masked = t_valid < c
    rmask = lax.broadcasted_iota(jnp.int32, (c, 1), 0) < t_valid

    g_all = -jnp.exp(alog_ref[...]) * _softplus(ab + dtb_ref[...])
    beta_all = _sigmoid(ab)
    if masked:
        g_all = jnp.where(rmask, g_all, 0.0)
        beta_all = jnp.where(rmask, beta_all, 0.0)

    row = lax.broadcasted_iota(jnp.int32, (c, c), 0)
    col = lax.broadcasted_iota(jnp.int32, (c, c), 1)
    incl = row >= col
    strict = row > col
    eye = (row == col).astype(f32)
    gc_all = jnp.dot(incl.astype(f32), g_all, precision=HI, preferred_element_type=f32)
    egc_all = jnp.exp(gc_all)
    lane = lax.broadcasted_iota(jnp.int32, (c, LANES), 1)
    ones = jnp.ones((c, LANES), f32)
    ng = ng_ref[...]

    for h in range(H_B):
        q = y[:, h * HEAD_DIM:(h + 1) * HEAD_DIM]
        k = y[:, D_B + h * HEAD_DIM:D_B + (h + 1) * HEAD_DIM]
        v = y[:, 2 * D_B + h * HEAD_DIM:2 * D_B + (h + 1) * HEAD_DIM]
        q = q * lax.rsqrt(jnp.sum(q * q, axis=-1, keepdims=True) + EPS) * (HEAD_DIM ** -0.5)
        k = k * lax.rsqrt(jnp.sum(k * k, axis=-1, keepdims=True) + EPS)
        if masked:
            q = jnp.where(rmask, q, 0.0)
            k = jnp.where(rmask, k, 0.0)
            v = jnp.where(rmask, v, 0.0)
        gc = gc_all[:, h:h + 1]
        egc = egc_all[:, h:h + 1]
        beta = beta_all[:, H_B + h:H_B + h + 1]
        gc_row = lax.dot_general(ones, jnp.where(lane == h, gc_all, 0.0), NT, precision=HI,
                                 preferred_element_type=f32)
        decay = jnp.exp(jnp.where(incl, gc - gc_row, NEG_BIG))
        kb = k * beta
        vb = v * beta
        k16 = k.astype(bf16)
        a = lax.dot_general(kb.astype(bf16), k16, NT, preferred_element_type=f32) * jnp.where(strict, decay, 0.0)
        pw = -a
        tinv = eye + pw
        for _ in range(int(math.log2(c)) - 1):
            pw = jnp.dot(pw, pw, precision=HI, preferred_element_type=f32)
            tinv = tinv + jnp.dot(pw, tinv, precision=HI, preferred_element_type=f32)
        rhs = jnp.concatenate([vb, kb * egc], axis=1)
        sol = jnp.dot(tinv, rhs, precision=HI, preferred_element_type=f32)
        uu = sol[:, :HEAD_DIM]
        ww = sol[:, HEAD_DIM:]
        qk = lax.dot_general(q.astype(bf16), k16, NT, preferred_element_type=f32) * decay
        s = s_ref[h]
        s16 = s.astype(bf16)
        v_new = uu - jnp.dot(ww.astype(bf16), s16, preferred_element_type=f32)
        vn16 = v_new.astype(bf16)
        o = (jnp.dot((q * egc).astype(bf16), s16, preferred_element_type=f32)
             + jnp.dot(qk.astype(bf16), vn16, preferred_element_type=f32))
        g_last = gc_all[c - 1:c, h:h + 1]
        kd = k * jnp.exp(g_last - gc)
        s_ref[h] = s * jnp.exp(g_last) + lax.dot_general(kd.astype(bf16), vn16, TN, preferred_element_type=f32)
        zz = z[:, h * HEAD_DIM:(h + 1) * HEAD_DIM]
        o = _rms(o, ng) * (zz * _sigmoid(zz))
        o_ref[:, h * HEAD_DIM:(h + 1) * HEAD_DIM] = o[:rows].astype(o_ref.dtype)


def _delta(p, pab, hist, s0, cw, alog, dtb, ng, nseq, rows, t_valid, nchunks, out_dtype):
    m = p.shape[0]
    kern = functools.partial(_delta_kernel, rows=rows, t_valid=t_valid, nchunks=nchunks)
    return pl.pallas_call(
        kern,
        grid=(nseq, nchunks),
        in_specs=[
            pl.BlockSpec((rows, 3 * D_B), lambda b, c: (b * nchunks + c, P_BQKV // (3 * D_B))),
            pl.BlockSpec((rows, D_B), lambda b, c: (b * nchunks + c, P_BZ // D_B)),
            pl.BlockSpec((rows, LANES), lambda b, c: (b * nchunks + c, 0)),
            pl.BlockSpec((None, CONV_B - 1, 3 * D_B), lambda b, c: (b, 0, 0)),
            pl.BlockSpec((None, H_B, HEAD_DIM, HEAD_DIM), lambda b, c: (b, 0, 0, 0)),
            pl.BlockSpec((CONV_B, 3 * D_B), lambda b, c: (0, 0)),
            pl.BlockSpec((1, LANES), lambda b, c: (0, 0)),
            pl.BlockSpec((1, LANES), lambda b, c: (0, 0)),
            pl.BlockSpec((1, HEAD_DIM), lambda b, c: (0, 0)),
        ],
        out_specs=[
            pl.BlockSpec((rows, D_B), lambda b, c: (b * nchunks + c, 0)),
            pl.BlockSpec((None, H_B, HEAD_DIM, HEAD_DIM), lambda b, c: (b, 0, 0, 0)),
            pl.BlockSpec((None, CONV_B - 1, 3 * D_B), lambda b, c: (b, 0, 0)),
        ],
        out_shape=[
            jax.ShapeDtypeStruct((m, D_B), out_dtype),
            jax.ShapeDtypeStruct((nseq, H_B, HEAD_DIM, HEAD_DIM), f32),
            jax.ShapeDtypeStruct((nseq, CONV_B - 1, 3 * D_B), f32),
        ],
        scratch_shapes=[pltpu.VMEM((rows + 8, 3 * D_B), f32)],
        compiler_params=_params("parallel", "arbitrary"),
        name="delta",
    )(p, p, pab, hist, s0, cw, alog, dtb, ng)


def _lam(lam_ref, lam_init):
    lv = lam_ref[...]
    a = jnp.sum(lv[0:1] * lv[1:2], axis=-1, keepdims=True)
    b = jnp.sum(lv[2:3] * lv[3:4], axis=-1, keepdims=True)
    return jnp.exp(a) - jnp.exp(b) + lam_init


def _attn_kernel(lam_ref, q_ref, k_ref, v_ref, g_ref, o_ref, qs, m_s, l_s, acc, *, t, lam_init):
    qi = pl.program_id(2)
    ki = pl.program_id(3)

    @pl.when(ki == 0)
    def _():
        q = q_ref[...] * (DC_HALF ** -0.5)
        lane = lax.broadcasted_iota(jnp.int32, q.shape, 1)
        qs[0:t, :] = jnp.where(lane < DC_HALF, q, 0.0).astype(bf16)
        qs[t:2 * t, :] = jnp.where(lane >= DC_HALF, q, 0.0).astype(bf16)
        m_s[...] = jnp.full(m_s.shape, NEG_BIG, f32)
        l_s[...] = jnp.zeros(l_s.shape, f32)
        acc[...] = jnp.zeros(acc.shape, f32)

    def step(diag):
        s = lax.dot_general(qs[...], k_ref[...].astype(bf16), NT, preferred_element_type=f32)
        if diag:
            row = lax.broadcasted_iota(jnp.int32, s.shape, 0)
            col = lax.broadcasted_iota(jnp.int32, s.shape, 1)
            s = jnp.where(col <= jnp.bitwise_and(row, t - 1), s, NEG_BIG)
        m_old = m_s[...]
        m_new = jnp.maximum(m_old, jnp.max(s, axis=-1, keepdims=True))
        alpha = jnp.exp(m_old - m_new)
        p = jnp.exp(s - m_new)
        l_s[...] = alpha * l_s[...] + jnp.sum(p, axis=-1, keepdims=True)
        acc[...] = alpha * acc[...] + jnp.dot(p.astype(bf16), v_ref[...].astype(bf16), preferred_element_type=f32)
        m_s[...] = m_new

    @pl.when(ki < qi)
    def _():
        step(False)

    @pl.when(ki == qi)
    def _():
        step(True)

    @pl.when(ki == pl.num_programs(3) - 1)
    def _():
        o12 = acc[...] / l_s[...]
        o = o12[0:t] - _lam(lam_ref, lam_init) * o12[t:2 * t]
        o = _rms(o, g_ref[...]) * (1.0 - lam_init)
        o_ref[...] = o.astype(o_ref.dtype)


def _attn(p, lamv, g, nseq, seq, t, lam_init):
    m = p.shape[0]
    nb = seq // t
    cq, ck, cv = P_CQ // HEAD_DIM, P_CK // HEAD_DIM, P_CV // HEAD_DIM
    kern = functools.partial(_attn_kernel, t=t, lam_init=lam_init)
    return pl.pallas_call(
        kern,
        grid=(nseq, H_C, nb, nb),
        in_specs=[
            pl.BlockSpec((4, DC_HALF), lambda b, h, i, j: (0, 0)),
            pl.BlockSpec((t, HEAD_DIM), lambda b, h, i, j: (b * nb + i, cq + h)),
            pl.BlockSpec((t, HEAD_DIM), lambda b, h, i, j: (b * nb + jnp.minimum(i, j), ck + h)),
            pl.BlockSpec((t, HEAD_DIM), lambda b, h, i, j: (b * nb + jnp.minimum(i, j), cv + h)),
            pl.BlockSpec((1, HEAD_DIM), lambda b, h, i, j: (0, 0)),
        ],
        out_specs=pl.BlockSpec((t, HEAD_DIM), lambda b, h, i, j: (b * nb + i, h)),
        out_shape=jax.ShapeDtypeStruct((m, D_C), bf16),
        scratch_shapes=[
            pltpu.VMEM((2 * t, HEAD_DIM), bf16),
            pltpu.VMEM((2 * t, 1), f32),
            pltpu.VMEM((2 * t, 1), f32),
            pltpu.VMEM((2 * t, HEAD_DIM), f32),
        ],
        compiler_params=_params("parallel", "parallel", "parallel", "arbitrary"),
        name="attn",
    )(lamv, p, p, p, g)


def _paged_attn_kernel(pt_ref, lam_ref, q_ref, kn_ref, vn_ref, g_ref, *rest, npg, lam_init):
    k_refs = rest[:npg]
    v_refs = rest[npg:2 * npg]
    o_ref = rest[2 * npg]
    qbd, kb, vb, m_s, l_s, acc = rest[2 * npg + 1:]
    gi = pl.program_id(1)
    nq = 2 * H_C * SAMPLE_ROWS

    @pl.when(gi == 0)
    def _():
        q = q_ref[...] * (DC_HALF ** -0.5)
        grp = lax.shift_right_logical(lax.broadcasted_iota(jnp.int32, q.shape, 1), int(math.log2(DC_HALF)))
        qbd[...] = jnp.concatenate([jnp.where(grp == j, q, 0.0) for j in range(2 * H_C)], axis=0).astype(bf16)
        m_s[...] = jnp.full(m_s.shape, NEG_BIG, f32)
        l_s[...] = jnp.zeros(l_s.shape, f32)
        acc[...] = jnp.zeros(acc.shape, f32)

    def update(s, v16):
        m_old = m_s[...]
        m_new = jnp.maximum(m_old, jnp.max(s, axis=-1, keepdims=True))
        alpha = jnp.exp(m_old - m_new)
        p = jnp.exp(s - m_new)
        l_s[...] = alpha * l_s[...] + jnp.sum(p, axis=-1, keepdims=True)
        acc[...] = alpha * acc[...] + jnp.dot(p.astype(bf16), v16, preferred_element_type=f32)
        m_s[...] = m_new

    for i in range(npg):
        kb[i * PAGE_SIZE:(i + 1) * PAGE_SIZE, :] = k_refs[i][...].astype(bf16)
        vb[i * PAGE_SIZE:(i + 1) * PAGE_SIZE, :] = v_refs[i][...].astype(bf16)
    update(lax.dot_general(qbd[...], kb[...], NT, preferred_element_type=f32), vb[...])

    @pl.when(gi == pl.num_programs(1) - 1)
    def _():
        pad = jnp.zeros((LANES - SAMPLE_ROWS, D_C), f32)
        kn = jnp.concatenate([kn_ref[...], pad], axis=0).astype(bf16)
        vn = jnp.concatenate([vn_ref[...], pad], axis=0).astype(bf16)
        s = lax.dot_general(qbd[...], kn, NT, preferred_element_type=f32)
        row = lax.broadcasted_iota(jnp.int32, s.shape, 0)
        col = lax.broadcasted_iota(jnp.int32, s.shape, 1)
        s = jnp.where(col <= jnp.bitwise_and(row, SAMPLE_ROWS - 1), s, NEG_BIG)
        update(s, vn)
        o12 = acc[...] / l_s[...]
        lam = _lam(lam_ref, lam_init)
        for h in range(H_C):
            r1 = 2 * h * SAMPLE_ROWS
            r2 = r1 + SAMPLE_ROWS
            cols = slice(h * HEAD_DIM, (h + 1) * HEAD_DIM)
            o = o12[r1:r1 + SAMPLE_ROWS, cols] - lam * o12[r2:r2 + SAMPLE_ROWS, cols]
            o_ref[:, cols] = _rms(o, g_ref[...]) * (1.0 - lam_init)
    del nq


def _paged_attn(p, lamv, g, cache_k, cache_v, page_table, layer, npg, lam_init):
    m = p.shape[0]
    nseq = m // SAMPLE_ROWS
    n_pages = page_table.shape[0] // nseq
    ngroups = n_pages // npg
    nq = 2 * H_C * SAMPLE_ROWS

    def page_spec(i):
        return pl.BlockSpec((None, None, PAGE_SIZE, D_C),
                            lambda b, gq, pt: (pt[b * n_pages + gq * npg + i], layer, 0, 0))

    row_spec = lambda c: pl.BlockSpec((SAMPLE_ROWS, D_C), lambda b, gq, pt: (b, c))
    kern = functools.partial(_paged_attn_kernel, npg=npg, lam_init=lam_init)
    grid_spec = pltpu.PrefetchScalarGridSpec(
        num_scalar_prefetch=1,
        grid=(nseq, ngroups),
        in_specs=[
            pl.BlockSpec((4, DC_HALF), lambda b, gq, pt: (0, 0)),
            row_spec(P_CQ // D_C), row_spec(P_CK // D_C), row_spec(P_CV // D_C),
            pl.BlockSpec((1, HEAD_DIM), lambda b, gq, pt: (0, 0)),
        ] + [page_spec(i) for i in range(npg)] + [page_spec(i) for i in range(npg)],
        out_specs=pl.BlockSpec((SAMPLE_ROWS, D_C), lambda b, gq, pt: (b, 0)),
        scratch_shapes=[
            pltpu.VMEM((nq, D_C), bf16),
            pltpu.VMEM((npg * PAGE_SIZE, D_C), bf16),
            pltpu.VMEM((npg * PAGE_SIZE, D_C), bf16),
            pltpu.VMEM((nq, 1), f32),
            pltpu.VMEM((nq, 1), f32),
            pltpu.VMEM((nq, D_C), f32),
        ],
    )
    return pl.pallas_call(
        kern,
        grid_spec=grid_spec,
        out_shape=jax.ShapeDtypeStruct((m, D_C), f32),
        compiler_params=_params("parallel", "arbitrary"),
        name="paged_attn",
    )(page_table, lamv, p, p, p, g, *([cache_k] * npg), *([cache_v] * npg))


def _out_proj_kernel(a_ref, b_ref, c_ref, x_ref, wa_ref, wb_ref, wc_ref, g_ref, xo_ref, h_ref):
    acc = jnp.dot(a_ref[...].astype(bf16), wa_ref[...], preferred_element_type=f32)
    acc += jnp.dot(b_ref[...].astype(bf16), wb_ref[...], preferred_element_type=f32)
    acc += jnp.dot(c_ref[...].astype(bf16), wc_ref[...], preferred_element_type=f32)
    xo = x_ref[...] + acc
    xo_ref[...] = xo
    h_ref[...] = _rms(xo, g_ref[...]).astype(bf16)


def _out_proj(a, b, c, x, wo, g, tm):
    m = x.shape[0]
    row = lambda w: pl.BlockSpec((tm, w), lambda i: (i, 0))
    return pl.pallas_call(
        _out_proj_kernel,
        grid=(m // tm,),
        in_specs=[
            row(D_A), row(D_B), row(D_C), row(D_MODEL),
            pl.BlockSpec((D_A, D_MODEL), lambda i: (0, 0)),
            pl.BlockSpec((D_B, D_MODEL), lambda i: (0, 0)),
            pl.BlockSpec((D_C, D_MODEL), lambda i: (0, 0)),
            pl.BlockSpec((1, D_MODEL), lambda i: (0, 0)),
        ],
        out_specs=[row(D_MODEL), row(D_MODEL)],
        out_shape=[jax.ShapeDtypeStruct((m, D_MODEL), f32), jax.ShapeDtypeStruct((m, D_MODEL), bf16)],
        compiler_params=_params("parallel"),
        name="out_proj",
    )(a, b, c, x, wo[:D_A], wo[D_A:D_A + D_B], wo[D_A + D_B:], g)


def _ffn_up_kernel(h_ref, wg_ref, wv_ref, cwg_ref, cwv_ref, cbg_ref, cbv_ref, hg_ref, hv_ref,
                   act_ref, ng_ref, nv_ref, *, nseq, seq_rows, t_valid):
    h = h_ref[...]
    tf = wg_ref.shape[1]
    rin = lax.broadcasted_iota(jnp.int32, (nseq, seq_rows, tf), 1)

    def conv(w_ref, cw_ref, cb_ref, hist_ref, new_ref):
        up = jnp.dot(h, w_ref[...], preferred_element_type=f32)
        up3 = up.reshape(nseq, seq_rows, tf)
        r1 = pltpu.roll(up, 1, axis=0).reshape(nseq, seq_rows, tf)
        r2 = pltpu.roll(up, 2, axis=0).reshape(nseq, seq_rows, tf)
        hist = hist_ref[...]
        h0 = hist[:, 0:1, :]
        h1 = hist[:, 1:2, :]
        prev1 = jnp.where(rin == 0, h1, r1)
        prev2 = jnp.where(rin == 0, h0, jnp.where(rin == 1, h1, r2))
        cw = cw_ref[...]
        y = prev2 * cw[0:1] + prev1 * cw[1:2] + up3 * cw[2:3] + cb_ref[...]
        new_ref[...] = up3[:, t_valid - 2:t_valid, :]
        return y

    yg = conv(wg_ref, cwg_ref, cbg_ref, hg_ref, ng_ref)
    yv = conv(wv_ref, cwv_ref, cbv_ref, hv_ref, nv_ref)
    act = yg * _sigmoid(yg) * yv
    act_ref[...] = act.reshape(nseq * seq_rows, tf).astype(bf16)


def _ffn_up(h2, w_up, cw, cb, hist, nseq_blk, seq_rows, t_valid, tf):
    m = h2.shape[0]
    tm = nseq_blk * seq_rows
    nseq = m // seq_rows
    nj = D_FF // tf
    kern = functools.partial(_ffn_up_kernel, nseq=nseq_blk, seq_rows=seq_rows, t_valid=t_valid)
    gate = lambda i, j: (0, j)
    val = lambda i, j: (0, nj + j)
    return pl.pallas_call(
        kern,
        grid=(m // tm, nj),
        in_specs=[
            pl.BlockSpec((tm, D_MODEL), lambda i, j: (i, 0)),
            pl.BlockSpec((D_MODEL, tf), gate),
            pl.BlockSpec((D_MODEL, tf), val),
            pl.BlockSpec((CONV_FFN, tf), gate),
            pl.BlockSpec((CONV_FFN, tf), val),
            pl.BlockSpec((1, tf), gate),
            pl.BlockSpec((1, tf), val),
            pl.BlockSpec((nseq_blk, CONV_FFN - 1, tf), lambda i, j: (i, 0, j)),
            pl.BlockSpec((nseq_blk, CONV_FFN - 1, tf), lambda i, j: (i, 0, nj + j)),
        ],
        out_specs=[
            pl.BlockSpec((tm, tf), lambda i, j: (i, j)),
            pl.BlockSpec((nseq_blk, CONV_FFN - 1, tf), lambda i, j: (i, 0, j)),
            pl.BlockSpec((nseq_blk, CONV_FFN - 1, tf), lambda i, j: (i, 0, j)),
        ],
        out_shape=[
            jax.ShapeDtypeStruct((m, D_FF), bf16),
            jax.ShapeDtypeStruct((nseq, CONV_FFN - 1, D_FF), f32),
            jax.ShapeDtypeStruct((nseq, CONV_FFN - 1, D_FF), f32),
        ],
        compiler_params=_params("parallel", "arbitrary"),
        name="ffn_up",
    )(h2, w_up, w_up, cw, cw, cb, cb, hist, hist)


def _ffn_down_kernel(a_ref, w_ref, x_ref, g_ref, o_ref, acc, *, final_norm):
    kk = pl.program_id(1)

    @pl.when(kk == 0)
    def _():
        acc[...] = x_ref[...]

    acc[...] += jnp.dot(a_ref[...], w_ref[...], preferred_element_type=f32)

    @pl.when(kk == pl.num_programs(1) - 1)
    def _():
        xo = acc[...]
        o_ref[...] = _rms(xo, g_ref[...]) if final_norm else xo


def _ffn_down(act, w_down, x, g, tm, tk, final_norm):
    m = x.shape[0]
    kern = functools.partial(_ffn_down_kernel, final_norm=final_norm)
    return pl.pallas_call(
        kern,
        grid=(m // tm, D_FF // tk),
        in_specs=[
            pl.BlockSpec((tm, tk), lambda i, k: (i, k)),
            pl.BlockSpec((tk, D_MODEL), lambda i, k: (k, 0)),
            pl.BlockSpec((tm, D_MODEL), lambda i, k: (i, 0)),
            pl.BlockSpec((1, D_MODEL), lambda i, k: (0, 0)),
        ],
        out_specs=pl.BlockSpec((tm, D_MODEL), lambda i, k: (i, 0)),
        out_shape=jax.ShapeDtypeStruct((m, D_MODEL), f32),
        scratch_shapes=[pltpu.VMEM((tm, D_MODEL), f32)],
        compiler_params=_params("parallel", "arbitrary"),
        name="ffn_down",
    )(act, w_down, x, g)


def _pad_lanes(v, offset=0):
    return jnp.zeros((1, LANES), f32).at[0, offset:offset + v.shape[0]].set(v)


def _prep_weights(w):
    out = []
    for l in range(DEPTH):
        wi = w["w_in"][l]
        w_main = jnp.concatenate(
            [wi[:, OFF_BQKV:OFF_BA], wi[:, OFF_CQ:N_IN], wi[:, OFF_AU:OFF_BQKV]], axis=1).astype(bf16)
        w_ab = jnp.pad(wi[:, OFF_BA:OFF_CQ], ((0, 0), (0, LANES - 2 * H_B))).astype(bf16)
        out.append(dict(
            norm1_g=w["norm1_g"][l][None],
            w_main=w_main, w_ab=w_ab,
            a_norm_g=w["a_norm_g"][l].reshape(1, D_A),
            a_ws=w["a_ws"][l], a_bs=w["a_bs"][l],
            b_conv_w=w["b_conv_w"][l],
            alog=_pad_lanes(w["b_a_log"][l]), dtb=_pad_lanes(w["b_dt_bias"][l]),
            b_norm_g=w["b_norm_g"][l][None],
            lamv=jnp.stack([w["c_lam_q1"][l], w["c_lam_k1"][l], w["c_lam_q2"][l], w["c_lam_k2"][l]]),
            c_norm_g=w["c_norm_g"][l][None],
            w_out=w["w_out"][l].astype(bf16),
            norm2_g=w["norm2_g"][l][None],
            w_up=w["w_up"][l].astype(bf16),
            ffn_conv_w=w["ffn_conv_w"][l], ffn_conv_b=w["ffn_conv_b"][l][None],
            w_down=w["w_down"][l].astype(bf16),
        ))
    return out


def _run_trunk(x, lw, final_g, nseq, seq_rows, t_valid, past):
    m = x.shape[0]
    sample = past is not None
    tm = min(512, m)
    k_rows, v_rows, deltas, convs, ffns, chunk_vs = [], [], [], [], [], []
    for l in range(DEPTH):
        w = lw[l]
        lam_init = 0.8 - 0.6 * math.exp(-0.3 * l)
        p, pab = _in_proj(x, w["norm1_g"], w["w_main"], w["w_ab"], tm, N_MAIN // 5)
        if sample:
            r = m
            a_w = jnp.tile(w["a_ws"][:, :seq_rows, :seq_rows], (1, nseq, nseq))
            a_b = jnp.tile(w["a_bs"][:, :seq_rows].T, (nseq, 1))
            oa, va = _chunk_mlp(p, w["a_norm_g"], a_w, jnp.pad(a_b, ((0, 0), (0, LANES - H_A))),
                                r, seq_rows, m, f32)
            ob, s_new, conv_new = _delta(
                p, pab, past["state_conv_qkv"][:, l], past["state_delta"][:, l], w["b_conv_w"], w["alog"],
                w["dtb"], w["b_norm_g"], nseq, seq_rows, t_valid, 1, f32)
            oc = _paged_attn(p, w["lamv"], w["c_norm_g"], past["cache_k"], past["cache_v"],
                             past["page_table"], l, 8, lam_init)
            ffn_hist = past["state_ffn_conv"][:, l]
            nseq_blk = nseq
        else:
            a_b = jnp.pad(w["a_bs"].T, ((0, 0), (0, LANES - H_A)))
            oa, va = _chunk_mlp(p, w["a_norm_g"], w["a_ws"], a_b, CHUNK_A, CHUNK_A, min(512, m), bf16)
            nchunks = seq_rows // CHUNK_B
            ob, s_new, conv_new = _delta(
                p, pab, jnp.zeros((nseq, CONV_B - 1, 3 * D_B), f32),
                jnp.zeros((nseq, H_B, HEAD_DIM, HEAD_DIM), f32), w["b_conv_w"], w["alog"], w["dtb"],
                w["b_norm_g"], nseq, CHUNK_B, CHUNK_B, nchunks, bf16)
            oc = _attn(p, w["lamv"], w["c_norm_g"], nseq, seq_rows, min(512, seq_rows), lam_init)
            ffn_hist = jnp.zeros((nseq, CONV_FFN - 1, 2 * D_FF), f32)
            nseq_blk = 1
        x, h2 = _out_proj(oa, ob, oc, x, w["w_out"], w["norm2_g"], tm)
        act, hg, hv = _ffn_up(h2, w["w_up"], w["ffn_conv_w"], w["ffn_conv_b"], ffn_hist,
                              nseq_blk, seq_rows, t_valid, 256)
        x = _ffn_down(act, w["w_down"], x, final_g[None], tm, 1408, l == DEPTH - 1)
        p3 = p.reshape(nseq, seq_rows, N_MAIN)[:, :t_valid]
        k_rows.append(p3[:, :, P_CK:P_CK + D_C].reshape(nseq, t_valid, H_C, HEAD_DIM))
        v_rows.append(p3[:, :, P_CV:P_CV + D_C].reshape(nseq, t_valid, H_C, HEAD_DIM))
        deltas.append(s_new)
        convs.append(conv_new)
        ffns.append(jnp.concatenate([hg, hv], axis=-1))
        if sample:
            chunk_vs.append(va.reshape(nseq, seq_rows, H_A, HEAD_DIM)[:, :t_valid])
    y = x.reshape(nseq, seq_rows, D_MODEL)[:, :t_valid]
    stack = lambda xs: jnp.stack(xs, axis=1)
    return (y, stack(k_rows), stack(v_rows), stack(deltas), stack(convs), stack(ffns),
            stack(chunk_vs) if sample else None)


def kernel(x_prompt, x_sample, cache_k, cache_v, page_table, state_delta, state_conv_qkv, state_ffn_conv,
           norm1_g, w_in, a_norm_g, a_ws, a_bs, b_conv_w, b_a_log, b_dt_bias, b_norm_g, c_lam_q1, c_lam_k1,
           c_lam_q2, c_lam_k2, c_norm_g, w_out, norm2_g, w_up, ffn_conv_w, ffn_conv_b, w_down, final_g):
    weights = dict(norm1_g=norm1_g, w_in=w_in, a_norm_g=a_norm_g, a_ws=a_ws, a_bs=a_bs, b_conv_w=b_conv_w,
                   b_a_log=b_a_log, b_dt_bias=b_dt_bias, b_norm_g=b_norm_g, c_lam_q1=c_lam_q1,
                   c_lam_k1=c_lam_k1, c_lam_q2=c_lam_q2, c_lam_k2=c_lam_k2, c_norm_g=c_norm_g, w_out=w_out,
                   norm2_g=norm2_g, w_up=w_up, ffn_conv_w=ffn_conv_w, ffn_conv_b=ffn_conv_b, w_down=w_down)
    lw = _prep_weights(weights)

    batch, seq, _ = x_prompt.shape
    yp, pk, pv, pd, pc, pf, _ = _run_trunk(
        x_prompt.reshape(batch * seq, D_MODEL), lw, final_g, batch, seq, seq, None)

    dbatch, dseq, _ = x_sample.shape
    xs = jnp.pad(x_sample, ((0, 0), (0, SAMPLE_ROWS - dseq), (0, 0))).reshape(dbatch * SAMPLE_ROWS, D_MODEL)
    n_pool = cache_k.shape[0]
    past = dict(
        cache_k=cache_k.reshape(n_pool, DEPTH, PAGE_SIZE, D_C),
        cache_v=cache_v.reshape(n_pool, DEPTH, PAGE_SIZE, D_C),
        page_table=page_table.reshape(-1).astype(jnp.int32),
        state_delta=state_delta, state_conv_qkv=state_conv_qkv, state_ffn_conv=state_ffn_conv)
    ys, sk, sv, sd, sc, sf, scv = _run_trunk(xs, lw, final_g, dbatch, SAMPLE_ROWS, dseq, past)
    return (yp, ys, pk, pv, pd, pc, pf, sk, sv, sd, sc, sf, scv)
```

```python
import functools
import math

import jax
import jax.numpy as jnp
from jax import lax
from jax.experimental import pallas as pl
from jax.experimental.pallas import tpu as pltpu

f32 = jnp.float32
bf16 = jnp.bfloat16

D_MODEL = 2048
DEPTH = 4
HEAD_DIM = 128
H_A = 4
D_A = H_A * HEAD_DIM
CHUNK_A = 128
H_B = 6
D_B = H_B * HEAD_DIM
H_C = 6
D_C = H_C * HEAD_DIM
DC_HALF = HEAD_DIM // 2
CONV_B = 4
CHUNK_B = 64
D_FF = 5632
CONV_FFN = 3
EPS = 1e-6
PAGE_SIZE = 128
SAMPLE_ROWS = 8

OFF_AU = 0
OFF_AV = OFF_AU + D_A
OFF_BQKV = OFF_AV + D_A
OFF_BZ = OFF_BQKV + 3 * D_B
OFF_BA = OFF_BZ + D_B
OFF_BB = OFF_BA + H_B
OFF_CQ = OFF_BB + H_B
OFF_CK = OFF_CQ + D_C
OFF_CV = OFF_CK + D_C
N_IN = OFF_CV + D_C

P_BQKV = 0
P_BZ = P_BQKV + 3 * D_B
P_CQ = P_BZ + D_B
P_CK = P_CQ + D_C
P_CV = P_CK + D_C
P_AU = P_CV + D_C
P_AV = P_AU + D_A
N_MAIN = P_AV + D_A
LANES = 128

VMEM_LIMIT = 56 * 1024 * 1024
NEG_BIG = -1e30
HI = lax.Precision.HIGHEST
NT = (((1,), (1,)), ((), ()))
TN = (((0,), (0,)), ((), ()))


def _params(*sem):
    return pltpu.CompilerParams(dimension_semantics=sem, vmem_limit_bytes=VMEM_LIMIT)


def _sigmoid(x):
    return 1.0 / (1.0 + jnp.exp(-x))


def _gelu_tanh(x):
    c = math.sqrt(2.0 / math.pi)
    return 0.5 * x * (1.0 + jnp.tanh(c * (x + 0.044715 * (x * x * x))))


def _rms(x, g):
    return x * lax.rsqrt(jnp.mean(x * x, axis=-1, keepdims=True) + EPS) * g


def _in_proj_kernel(x_ref, g_ref, w_ref, wab_ref, p_ref, pab_ref, h_scr):
    @pl.when(pl.program_id(1) == 0)
    def _():
        h = _rms(x_ref[...], g_ref[...]).astype(bf16)
        h_scr[...] = h
        pab_ref[...] = jnp.dot(h, wab_ref[...], preferred_element_type=f32)

    p_ref[...] = jnp.dot(h_scr[...], w_ref[...], preferred_element_type=f32)


def _in_proj(x, g, w, wab, tm, tn):
    m = x.shape[0]
    n = w.shape[1]
    return pl.pallas_call(
        _in_proj_kernel,
        grid=(m // tm, n // tn),
        in_specs=[
            pl.BlockSpec((tm, D_MODEL), lambda i, j: (i, 0)),
            pl.BlockSpec((1, D_MODEL), lambda i, j: (0, 0)),
            pl.BlockSpec((D_MODEL, tn), lambda i, j: (0, j)),
            pl.BlockSpec((D_MODEL, LANES), lambda i, j: (0, 0)),
        ],
        out_specs=[
            pl.BlockSpec((tm, tn), lambda i, j: (i, j)),
            pl.BlockSpec((tm, LANES), lambda i, j: (i, 0)),
        ],
        out_shape=[jax.ShapeDtypeStruct((m, n), f32), jax.ShapeDtypeStruct((m, LANES), f32)],
        scratch_shapes=[pltpu.VMEM((tm, D_MODEL), bf16)],
        compiler_params=_params("parallel", "arbitrary"),
        name="in_proj",
    )(x, g, w, wab)


def _chunk_mlp_kernel(u0_ref, u1_ref, v0_ref, v1_ref, g_ref, w_ref, bs_ref, oa_ref, va_ref, *, r, seq_shift):
    nsub = u0_ref.shape[0] // r
    row = lax.broadcasted_iota(jnp.int32, (r, r), 0)
    col = lax.broadcasted_iota(jnp.int32, (r, r), 1)
    same_seq = lax.shift_right_logical(row, seq_shift) == lax.shift_right_logical(col, seq_shift)
    mask = jnp.logical_and(same_seq, col <= row)
    g = g_ref[...]
    for h in range(H_A):
        wm = jnp.where(mask, w_ref[h], 0.0).astype(bf16)
        u_ref = (u0_ref, u1_ref)[h // 2]
        v_ref = (v0_ref, v1_ref)[h // 2]
        lo = (h % 2) * HEAD_DIM
        for c in range(nsub):
            rows = slice(c * r, (c + 1) * r)
            u = _gelu_tanh(u_ref[rows, lo:lo + HEAD_DIM])
            v = _rms(_gelu_tanh(v_ref[rows, lo:lo + HEAD_DIM]), g[:, h * HEAD_DIM:(h + 1) * HEAD_DIM])
            va_ref[rows, h * HEAD_DIM:(h + 1) * HEAD_DIM] = v
            mixed = jnp.dot(wm, v.astype(bf16), preferred_element_type=f32) + bs_ref[:, h:h + 1]
            oa_ref[rows, h * HEAD_DIM:(h + 1) * HEAD_DIM] = (u * mixed).astype(oa_ref.dtype)


def _chunk_mlp(p, g, w, bs_rows, r, seq_rows, rb, out_dtype):
    m = p.shape[0]
    half = 2 * HEAD_DIM
    cu, cv = P_AU // half, P_AV // half
    kern = functools.partial(_chunk_mlp_kernel, r=r, seq_shift=int(math.log2(seq_rows)))
    return pl.pallas_call(
        kern,
        grid=(m // rb,),
        in_specs=[
            pl.BlockSpec((rb, half), lambda i: (i, cu)),
            pl.BlockSpec((rb, half), lambda i: (i, cu + 1)),
            pl.BlockSpec((rb, half), lambda i: (i, cv)),
            pl.BlockSpec((rb, half), lambda i: (i, cv + 1)),
            pl.BlockSpec((1, D_A), lambda i: (0, 0)),
            pl.BlockSpec((H_A, r, r), lambda i: (0, 0, 0)),
            pl.BlockSpec((r, LANES), lambda i: (0, 0)),
        ],
        out_specs=[
            pl.BlockSpec((rb, D_A), lambda i: (i, 0)),
            pl.BlockSpec((rb, D_A), lambda i: (i, 0)),
        ],
        out_shape=[jax.ShapeDtypeStruct((m, D_A), out_dtype), jax.ShapeDtypeStruct((m, D_A), f32)],
        compiler_params=_params("parallel"),
        name="chunk_mlp",
    )(p, p, p, p, g, w, bs_rows)


def _softplus(x):
    return jnp.maximum(x, 0.0) + jnp.log(1.0 + jnp.exp(-jnp.abs(x)))


def _mm(a, b):
    return jnp.dot(a.astype(bf16), b.astype(bf16), preferred_element_type=f32)


def _delta_kernel(x_ref, z_ref, ab_ref, hist_ref, s0_ref, cw_ref, alog_ref, dtb_ref, ng_ref,
                  o_ref, s_ref, cn_ref, ext, *, rows, t_valid, nchunks):
    c = CHUNK_B
    ci = pl.program_id(1)

    @pl.when(ci == 0)
    def _():
        ext[0:8, :] = jnp.zeros((8, 3 * D_B), f32)
        ext[5:8, :] = hist_ref[...]
        s_ref[...] = s0_ref[...]

    x = x_ref[...]
    ext[8:8 + rows, :] = x
    cw = cw_ref[...]
    y = (ext[5:5 + rows, :] * cw[0:1] + ext[6:6 + rows, :] * cw[1:2]
         + ext[7:7 + rows, :] * cw[2:3] + x * cw[3:4])

    @pl.when(ci == nchunks - 1)
    def _():
        cn_ref[...] = ext[5 + t_valid:8 + t_valid, :]

    ext[0:8, :] = ext[rows:rows + 8, :]

    y = y * _sigmoid(y)
    z = z_ref[...]
    ab = ab_ref[...]
    if rows < c:
        y = jnp.concatenate([y, jnp.zeros((c - rows, y.shape[1]), f32)], axis=0)
        z = jnp.concatenate([z, jnp.zeros((c - rows, z.shape[1]), f32)], axis=0)
        ab = jnp.concatenate([ab, jnp.zeros((c - rows, ab.shape[1]), f32)], axis=0)
    masked = t_valid < c
    rmask = lax.broadcasted_iota(jnp.int32, (c, 1), 0) < t_valid

    g_all = -jnp.exp(alog_ref[...]) * _softplus(ab + dtb_ref[...])
    beta_all = _sigmoid(ab)
    if masked:
        g_all = jnp.where(rmask, g_all, 0.0)
        beta_all = jnp.where(rmask, beta_all, 0.0)

    row = lax.broadcasted_iota(jnp.int32, (c, c), 0)
    col = lax.broadcasted_iota(jnp.int32, (c, c), 1)
    incl = row >= col
    strict = row > col
    eye = (row == col).astype(f32)
    gc_all = jnp.dot(incl.astype(f32), g_all, precision=HI, preferred_element_type=f32)
    egc_all = jnp.exp(gc_all)
    blk8 = lax.shift_right_logical(row, 3) == lax.shift_right_logical(col, 3)
    lvl_masks = []
    for sh in range(3, int(math.log2(c))):
        rb = lax.shift_right_logical(row, sh)
        cb = lax.shift_right_logical(col, sh)
        lvl_masks.append(jnp.logical_and(jnp.bitwise_and(rb, 1) == 1, cb == rb - 1))
    gc_t = jnp.transpose(jnp.concatenate([gc_all, jnp.zeros((LANES - c, LANES), f32)], axis=0))
    ng = ng_ref[...]

    hs = range(H_B)
    hd = lambda h, base=0: slice(base + h * HEAD_DIM, base + (h + 1) * HEAD_DIM)

    q = [y[:, hd(h)] for h in hs]
    k = [y[:, hd(h, D_B)] for h in hs]
    v = [y[:, hd(h, 2 * D_B)] for h in hs]
    q = [t * lax.rsqrt(jnp.sum(t * t, axis=-1, keepdims=True) + EPS) * (HEAD_DIM ** -0.5) for t in q]
    k = [t * lax.rsqrt(jnp.sum(t * t, axis=-1, keepdims=True) + EPS) for t in k]
    if masked:
        q = [jnp.where(rmask, t, 0.0) for t in q]
        k = [jnp.where(rmask, t, 0.0) for t in k]
        v = [jnp.where(rmask, t, 0.0) for t in v]
    gc = [gc_all[:, h:h + 1] for h in hs]
    egc = [egc_all[:, h:h + 1] for h in hs]
    beta = [beta_all[:, H_B + h:H_B + h + 1] for h in hs]
    decay = [jnp.exp(jnp.where(incl, gc[h] - gc_t[h:h + 1, 0:c], NEG_BIG)) for h in hs]
    kb = [k[h] * beta[h] for h in hs]
    vb = [v[h] * beta[h] for h in hs]
    k16 = [t.astype(bf16) for t in k]
    aq = [lax.dot_general(jnp.concatenate([kb[h], q[h]], axis=0).astype(bf16), k16[h], NT,
                          preferred_element_type=f32) for h in hs]
    a = [aq[h][:c] * jnp.where(strict, decay[h], 0.0) for h in hs]
    qk = [aq[h][c:] * decay[h] for h in hs]
    x = [jnp.where(blk8, -t, 0.0) for t in a]
    tinv = [eye + t for t in x]
    p1 = [_mm(t, t) for t in x]
    pt = [_mm(jnp.concatenate([p1[h], tinv[h]], axis=0), p1[h]) for h in hs]
    tinv = [tinv[h] + pt[h][c:] for h in hs]
    t2 = [_mm(tinv[h], pt[h][:c]) for h in hs]
    tinv = [tinv[h] + t2[h] for h in hs]
    for msk in lvl_masks:
        lt = [_mm(jnp.where(msk, a[h], 0.0), tinv[h]) for h in hs]
        tl = [_mm(tinv[h], lt[h]) for h in hs]
        tinv = [tinv[h] - tl[h] for h in hs]
    rhs = [jnp.concatenate([vb[h], kb[h] * egc[h]], axis=1) for h in hs]
    sol = [rhs[h] + _mm(tinv[h] - eye, rhs[h]) for h in hs]
    s = [s_ref[h] for h in hs]
    wq = [_mm(jnp.concatenate([sol[h][:, HEAD_DIM:], q[h] * egc[h]], axis=0), s[h]) for h in hs]
    vn16 = [(sol[h][:, :HEAD_DIM] - wq[h][:c]).astype(bf16) for h in hs]
    o = [wq[h][c:] + jnp.dot(qk[h].astype(bf16), vn16[h], preferred_element_type=f32) for h in hs]
    g_last = [gc_all[c - 1:c, h:h + 1] for h in hs]
    kd = [(k[h] * jnp.exp(g_last[h] - gc[h])).astype(bf16) for h in hs]
    ds = [lax.dot_general(kd[h], vn16[h], TN, preferred_element_type=f32) for h in hs]
    for h in hs:
        s_ref[h] = s[h] * jnp.exp(g_last[h]) + ds[h]
    for h in hs:
        zz = z[:, hd(h)]
        oh = _rms(o[h], ng) * (zz * _sigmoid(zz))
        o_ref[:, hd(h)] = oh[:rows].astype(o_ref.dtype)


def _delta(p, pab, hist, s0, cw, alog, dtb, ng, nseq, rows, t_valid, nchunks, out_dtype):
    m = p.shape[0]
    kern = functools.partial(_delta_kernel, rows=rows, t_valid=t_valid, nchunks=nchunks)
    return pl.pallas_call(
        kern,
        grid=(nseq, nchunks),
        in_specs=[
            pl.BlockSpec((rows, 3 * D_B), lambda b, c: (b * nchunks + c, P_BQKV // (3 * D_B))),
            pl.BlockSpec((rows, D_B), lambda b, c: (b * nchunks + c, P_BZ // D_B)),
            pl.BlockSpec((rows, LANES), lambda b, c: (b * nchunks + c, 0)),
            pl.BlockSpec((None, CONV_B - 1, 3 * D_B), lambda b, c: (b, 0, 0)),
            pl.BlockSpec((None, H_B, HEAD_DIM, HEAD_DIM), lambda b, c: (b, 0, 0, 0)),
            pl.BlockSpec((CONV_B, 3 * D_B), lambda b, c: (0, 0)),
            pl.BlockSpec((1, LANES), lambda b, c: (0, 0)),
            pl.BlockSpec((1, LANES), lambda b, c: (0, 0)),
            pl.BlockSpec((1, HEAD_DIM), lambda b, c: (0, 0)),
        ],
        out_specs=[
            pl.BlockSpec((rows, D_B), lambda b, c: (b * nchunks + c, 0)),
            pl.BlockSpec((None, H_B, HEAD_DIM, HEAD_DIM), lambda b, c: (b, 0, 0, 0)),
            pl.BlockSpec((None, CONV_B - 1, 3 * D_B), lambda b, c: (b, 0, 0)),
        ],
        out_shape=[
            jax.ShapeDtypeStruct((m, D_B), out_dtype),
            jax.ShapeDtypeStruct((nseq, H_B, HEAD_DIM, HEAD_DIM), f32),
            jax.ShapeDtypeStruct((nseq, CONV_B - 1, 3 * D_B), f32),
        ],
        scratch_shapes=[pltpu.VMEM((rows + 8, 3 * D_B), f32)],
        compiler_params=_params("parallel", "arbitrary"),
        name="delta",
    )(p, p, pab, hist, s0, cw, alog, dtb, ng)


def _lam(lam_ref, lam_init):
    lv = lam_ref[...]
    a = jnp.sum(lv[0:1] * lv[1:2], axis=-1, keepdims=True)
    b = jnp.sum(lv[2:3] * lv[3:4], axis=-1, keepdims=True)
    return jnp.exp(a) - jnp.exp(b) + lam_init


def _attn_kernel(lam_ref, q_ref, k_ref, v_ref, g_ref, o_ref, qs, m_s, l_s, acc, *, t, lam_init):
    qi = pl.program_id(2)
    q = q_ref[...] * (DC_HALF ** -0.5)
    lane = lax.broadcasted_iota(jnp.int32, q.shape, 1)
    qs[0] = jnp.where(lane < DC_HALF, q, 0.0).astype(bf16)
    qs[1] = jnp.where(lane >= DC_HALF, q, 0.0).astype(bf16)
    m_s[...] = jnp.full(m_s.shape, NEG_BIG, f32)
    l_s[...] = jnp.zeros(l_s.shape, f32)
    acc[...] = jnp.zeros(acc.shape, f32)

    def block(j, diag):
        off = pl.multiple_of(j * t, t)
        k = k_ref[pl.ds(off, t), :].astype(bf16)
        v = v_ref[pl.ds(off, t), :].astype(bf16)
        cs = range(2)
        s = [lax.dot_general(qs[c], k, NT, preferred_element_type=f32) for c in cs]
        if diag:
            row = lax.broadcasted_iota(jnp.int32, (t, t), 0)
            col = lax.broadcasted_iota(jnp.int32, (t, t), 1)
            s = [jnp.where(col <= row, sc, NEG_BIG) for sc in s]
        m_old = [m_s[c] for c in cs]
        m_new = [jnp.maximum(m_old[c], jnp.max(s[c], axis=-1, keepdims=True)) for c in cs]
        alpha = [jnp.exp(m_old[c] - m_new[c]) for c in cs]
        p = [jnp.exp(s[c] - jnp.concatenate([m_new[c]] * (t // HEAD_DIM), axis=1)) for c in cs]
        pv = [jnp.dot(p[c].astype(bf16), v, preferred_element_type=f32) for c in cs]
        for c in cs:
            l_s[c] = alpha[c] * l_s[c] + jnp.sum(p[c], axis=-1, keepdims=True)
            acc[c] = alpha[c] * acc[c] + pv[c]
            m_s[c] = m_new[c]

    def body(j, carry):
        block(j, False)
        return carry

    lax.fori_loop(0, qi, body, 0)
    block(qi, True)

    o = acc[0] / l_s[0] - _lam(lam_ref, lam_init) * (acc[1] / l_s[1])
    o = _rms(o, g_ref[...]) * (1.0 - lam_init)
    o_ref[...] = o.astype(o_ref.dtype)


def _attn(p, lamv, g, nseq, seq, t, lam_init):
    m = p.shape[0]
    nb = seq // t
    cq, ck, cv = P_CQ // HEAD_DIM, P_CK // HEAD_DIM, P_CV // HEAD_DIM
    kern = functools.partial(_attn_kernel, t=t, lam_init=lam_init)
    return pl.pallas_call(
        kern,
        grid=(nseq, H_C, nb),
        in_specs=[
            pl.BlockSpec((4, DC_HALF), lambda b, h, i: (0, 0)),
            pl.BlockSpec((t, HEAD_DIM), lambda b, h, i: (b * nb + i, cq + h)),
            pl.BlockSpec((seq, HEAD_DIM), lambda b, h, i: (b, ck + h)),
            pl.BlockSpec((seq, HEAD_DIM), lambda b, h, i: (b, cv + h)),
            pl.BlockSpec((1, HEAD_DIM), lambda b, h, i: (0, 0)),
        ],
        out_specs=pl.BlockSpec((t, HEAD_DIM), lambda b, h, i: (b * nb + i, h)),
        out_shape=jax.ShapeDtypeStruct((m, D_C), bf16),
        scratch_shapes=[
            pltpu.VMEM((2, t, HEAD_DIM), bf16),
            pltpu.VMEM((2, t, HEAD_DIM), f32),
            pltpu.VMEM((2, t, HEAD_DIM), f32),
            pltpu.VMEM((2, t, HEAD_DIM), f32),
        ],
        compiler_params=_params("parallel", "parallel", "arbitrary"),
        name="attn",
    )(lamv, p, p, p, g)


def _paged_attn_kernel(pt_ref, lam_ref, q_ref, kn_ref, vn_ref, g_ref, *rest, npg, lam_init):
    k_refs = rest[:npg]
    v_refs = rest[npg:2 * npg]
    o_ref = rest[2 * npg]
    qs, kb, vb, m_s, l_s, acc = rest[2 * npg + 1:]
    gi = pl.program_id(1)
    hq = 2 * SAMPLE_ROWS

    @pl.when(gi == 0)
    def _():
        q = q_ref[...] * (DC_HALF ** -0.5)
        lane = lax.broadcasted_iota(jnp.int32, (SAMPLE_ROWS, HEAD_DIM), 1)
        for h in range(H_C):
            qh = q[:, h * HEAD_DIM:(h + 1) * HEAD_DIM]
            qs[h] = jnp.concatenate([jnp.where(lane < DC_HALF, qh, 0.0),
                                     jnp.where(lane >= DC_HALF, qh, 0.0)], axis=0).astype(bf16)
        m_s[...] = jnp.full(m_s.shape, NEG_BIG, f32)
        l_s[...] = jnp.zeros(l_s.shape, f32)
        acc[...] = jnp.zeros(acc.shape, f32)

    def update(s, v_of_head):
        m_old = m_s[...]
        m_new = jnp.maximum(m_old, jnp.max(s, axis=-1, keepdims=True))
        alpha = jnp.exp(m_old - m_new)
        p = jnp.exp(s - jnp.concatenate([m_new] * (s.shape[1] // LANES), axis=1))
        l_s[...] = alpha * l_s[...] + jnp.sum(p, axis=-1, keepdims=True)
        p16 = p.astype(bf16)
        pv = jnp.concatenate(
            [jnp.dot(p16[h * hq:(h + 1) * hq], v_of_head(h), preferred_element_type=f32) for h in range(H_C)],
            axis=0)
        acc[...] = alpha * acc[...] + pv
        m_s[...] = m_new

    for i in range(npg):
        for h in range(H_C):
            kb[h, i * PAGE_SIZE:(i + 1) * PAGE_SIZE, :] = k_refs[i][h].astype(bf16)
            vb[h, i * PAGE_SIZE:(i + 1) * PAGE_SIZE, :] = v_refs[i][h].astype(bf16)
    s_all = jnp.concatenate(
        [lax.dot_general(qs[h], kb[h], NT, preferred_element_type=f32) for h in range(H_C)], axis=0)
    update(s_all, lambda h: vb[h])

    @pl.when(gi == pl.num_programs(1) - 1)
    def _():
        pad = jnp.zeros((LANES - SAMPLE_ROWS, HEAD_DIM), f32)
        kn = kn_ref[...]
        vn = vn_ref[...]
        head = lambda x, h: jnp.concatenate([x[:, h * HEAD_DIM:(h + 1) * HEAD_DIM], pad], axis=0).astype(bf16)
        s = jnp.concatenate(
            [lax.dot_general(qs[h], head(kn, h), NT, preferred_element_type=f32) for h in range(H_C)], axis=0)
        row = lax.broadcasted_iota(jnp.int32, s.shape, 0)
        col = lax.broadcasted_iota(jnp.int32, s.shape, 1)
        s = jnp.where(col <= jnp.bitwise_and(row, SAMPLE_ROWS - 1), s, NEG_BIG)
        update(s, lambda h: head(vn, h))
        o12 = acc[...] / l_s[...]
        lam = _lam(lam_ref, lam_init)
        for h in range(H_C):
            r1 = h * hq
            r2 = r1 + SAMPLE_ROWS
            o = o12[r1:r1 + SAMPLE_ROWS] - lam * o12[r2:r2 + SAMPLE_ROWS]
            o_ref[:, h * HEAD_DIM:(h + 1) * HEAD_DIM] = _rms(o, g_ref[...]) * (1.0 - lam_init)


def _paged_attn(p, lamv, g, cache_k, cache_v, page_table, layer, npg, lam_init):
    m = p.shape[0]
    nseq = m // SAMPLE_ROWS
    n_pages = page_table.shape[0] // nseq
    ngroups = n_pages // npg
    nq = 2 * H_C * SAMPLE_ROWS

    def page_spec(i):
        return pl.BlockSpec((None, None, H_C, PAGE_SIZE, HEAD_DIM),
                            lambda b, gq, pt: (pt[b * n_pages + gq * npg + i], layer, 0, 0, 0))

    row_spec = lambda c: pl.BlockSpec((SAMPLE_ROWS, D_C), lambda b, gq, pt: (b, c))
    kern = functools.partial(_paged_attn_kernel, npg=npg, lam_init=lam_init)
    grid_spec = pltpu.PrefetchScalarGridSpec(
        num_scalar_prefetch=1,
        grid=(nseq, ngroups),
        in_specs=[
            pl.BlockSpec((4, DC_HALF), lambda b, gq, pt: (0, 0)),
            row_spec(P_CQ // D_C), row_spec(P_CK // D_C), row_spec(P_CV // D_C),
            pl.BlockSpec((1, HEAD_DIM), lambda b, gq, pt: (0, 0)),
        ] + [page_spec(i) for i in range(npg)] + [page_spec(i) for i in range(npg)],
        out_specs=pl.BlockSpec((SAMPLE_ROWS, D_C), lambda b, gq, pt: (b, 0)),
        scratch_shapes=[
            pltpu.VMEM((H_C, 2 * SAMPLE_ROWS, HEAD_DIM), bf16),
            pltpu.VMEM((H_C, npg * PAGE_SIZE, HEAD_DIM), bf16),
            pltpu.VMEM((H_C, npg * PAGE_SIZE, HEAD_DIM), bf16),
            pltpu.VMEM((nq, LANES), f32),
            pltpu.VMEM((nq, LANES), f32),
            pltpu.VMEM((nq, HEAD_DIM), f32),
        ],
    )
    return pl.pallas_call(
        kern,
        grid_spec=grid_spec,
        out_shape=jax.ShapeDtypeStruct((m, D_C), f32),
        compiler_params=_params("parallel", "arbitrary"),
        name="paged_attn",
    )(page_table, lamv, p, p, p, g, *([cache_k] * npg), *([cache_v] * npg))


def _out_proj_kernel(a_ref, b_ref, c_ref, x_ref, wa_ref, wb_ref, wc_ref, g_ref, xo_ref, h_ref):
    acc = jnp.dot(a_ref[...].astype(bf16), wa_ref[...], preferred_element_type=f32)
    acc += jnp.dot(b_ref[...].astype(bf16), wb_ref[...], preferred_element_type=f32)
    acc += jnp.dot(c_ref[...].astype(bf16), wc_ref[...], preferred_element_type=f32)
    xo = x_ref[...] + acc
    xo_ref[...] = xo
    h_ref[...] = _rms(xo, g_ref[...]).astype(bf16)


def _out_proj(a, b, c, x, wo, g, tm):
    m = x.shape[0]
    row = lambda w: pl.BlockSpec((tm, w), lambda i: (i, 0))
    return pl.pallas_call(
        _out_proj_kernel,
        grid=(m // tm,),
        in_specs=[
            row(D_A), row(D_B), row(D_C), row(D_MODEL),
            pl.BlockSpec((D_A, D_MODEL), lambda i: (0, 0)),
            pl.BlockSpec((D_B, D_MODEL), lambda i: (0, 0)),
            pl.BlockSpec((D_C, D_MODEL), lambda i: (0, 0)),
            pl.BlockSpec((1, D_MODEL), lambda i: (0, 0)),
        ],
        out_specs=[row(D_MODEL), row(D_MODEL)],
        out_shape=[jax.ShapeDtypeStruct((m, D_MODEL), f32), jax.ShapeDtypeStruct((m, D_MODEL), bf16)],
        compiler_params=_params("parallel"),
        name="out_proj",
    )(a, b, c, x, wo[:D_A], wo[D_A:D_A + D_B], wo[D_A + D_B:], g)


def _ffn_up_kernel(h_ref, wg_ref, wv_ref, cwg_ref, cwv_ref, cbg_ref, cbv_ref, hg_ref, hv_ref,
                   act_ref, ng_ref, nv_ref, *, nseq, seq_rows, t_valid):
    h = h_ref[...]
    tf = wg_ref.shape[1]
    rin = lax.broadcasted_iota(jnp.int32, (nseq, seq_rows, tf), 1)

    def conv(w_ref, cw_ref, cb_ref, hist_ref, new_ref):
        up = jnp.dot(h, w_ref[...], preferred_element_type=f32)
        up3 = up.reshape(nseq, seq_rows, tf)
        r1 = pltpu.roll(up, 1, axis=0).reshape(nseq, seq_rows, tf)
        r2 = pltpu.roll(up, 2, axis=0).reshape(nseq, seq_rows, tf)
        hist = hist_ref[...]
        h0 = hist[:, 0:1, :]
        h1 = hist[:, 1:2, :]
        prev1 = jnp.where(rin == 0, h1, r1)
        prev2 = jnp.where(rin == 0, h0, jnp.where(rin == 1, h1, r2))
        cw = cw_ref[...]
        y = prev2 * cw[0:1] + prev1 * cw[1:2] + up3 * cw[2:3] + cb_ref[...]
        new_ref[...] = up3[:, t_valid - 2:t_valid, :]
        return y

    yg = conv(wg_ref, cwg_ref, cbg_ref, hg_ref, ng_ref)
    yv = conv(wv_ref, cwv_ref, cbv_ref, hv_ref, nv_ref)
    act = yg * _sigmoid(yg) * yv
    act_ref[...] = act.reshape(nseq * seq_rows, tf).astype(bf16)


def _ffn_up(h2, w_up, cw, cb, hist, nseq_blk, seq_rows, t_valid, tf):
    m = h2.shape[0]
    tm = nseq_blk * seq_rows
    nseq = m // seq_rows
    nj = D_FF // tf
    kern = functools.partial(_ffn_up_kernel, nseq=nseq_blk, seq_rows=seq_rows, t_valid=t_valid)
    gate = lambda i, j: (0, j)
    val = lambda i, j: (0, nj + j)
    return pl.pallas_call(
        kern,
        grid=(m // tm, nj),
        in_specs=[
            pl.BlockSpec((tm, D_MODEL), lambda i, j: (i, 0)),
            pl.BlockSpec((D_MODEL, tf), gate),
            pl.BlockSpec((D_MODEL, tf), val),
            pl.BlockSpec((CONV_FFN, tf), gate),
            pl.BlockSpec((CONV_FFN, tf), val),
            pl.BlockSpec((1, tf), gate),
            pl.BlockSpec((1, tf), val),
            pl.BlockSpec((nseq_blk, CONV_FFN - 1, tf), lambda i, j: (i, 0, j)),
            pl.BlockSpec((nseq_blk, CONV_FFN - 1, tf), lambda i, j: (i, 0, nj + j)),
        ],
        out_specs=[
            pl.BlockSpec((tm, tf), lambda i, j: (i, j)),
            pl.BlockSpec((nseq_blk, CONV_FFN - 1, tf), lambda i, j: (i, 0, j)),
            pl.BlockSpec((nseq_blk, CONV_FFN - 1, tf), lambda i, j: (i, 0, j)),
        ],
        out_shape=[
            jax.ShapeDtypeStruct((m, D_FF), bf16),
            jax.ShapeDtypeStruct((nseq, CONV_FFN - 1, D_FF), f32),
            jax.ShapeDtypeStruct((nseq, CONV_FFN - 1, D_FF), f32),
        ],
        compiler_params=_params("parallel", "arbitrary"),
        name="ffn_up",
    )(h2, w_up, w_up, cw, cw, cb, cb, hist, hist)


def _ffn_down_kernel(a_ref, w_ref, x_ref, g_ref, o_ref, acc, *, final_norm):
    kk = pl.program_id(1)

    @pl.when(kk == 0)
    def _():
        acc[...] = x_ref[...]

    acc[...] += jnp.dot(a_ref[...], w_ref[...], preferred_element_type=f32)

    @pl.when(kk == pl.num_programs(1) - 1)
    def _():
        xo = acc[...]
        o_ref[...] = _rms(xo, g_ref[...]) if final_norm else xo


def _ffn_down(act, w_down, x, g, tm, tk, final_norm):
    m = x.shape[0]
    kern = functools.partial(_ffn_down_kernel, final_norm=final_norm)
    return pl.pallas_call(
        kern,
        grid=(m // tm, D_FF // tk),
        in_specs=[
            pl.BlockSpec((tm, tk), lambda i, k: (i, k)),
            pl.BlockSpec((tk, D_MODEL), lambda i, k: (k, 0)),
            pl.BlockSpec((tm, D_MODEL), lambda i, k: (i, 0)),
            pl.BlockSpec((1, D_MODEL), lambda i, k: (0, 0)),
        ],
        out_specs=pl.BlockSpec((tm, D_MODEL), lambda i, k: (i, 0)),
        out_shape=jax.ShapeDtypeStruct((m, D_MODEL), f32),
        scratch_shapes=[pltpu.VMEM((tm, D_MODEL), f32)],
        compiler_params=_params("parallel", "arbitrary"),
        name="ffn_down",
    )(act, w_down, x, g)


def _pad_lanes(v, offset=0):
    return jnp.zeros((1, LANES), f32).at[0, offset:offset + v.shape[0]].set(v)


def _prep_weights(w):
    out = []
    for l in range(DEPTH):
        wi = w["w_in"][l]
        w_main = jnp.concatenate(
            [wi[:, OFF_BQKV:OFF_BA], wi[:, OFF_CQ:N_IN], wi[:, OFF_AU:OFF_BQKV]], axis=1).astype(bf16)
        w_ab = jnp.pad(wi[:, OFF_BA:OFF_CQ], ((0, 0), (0, LANES - 2 * H_B))).astype(bf16)
        out.append(dict(
            norm1_g=w["norm1_g"][l][None],
            w_main=w_main, w_ab=w_ab,
            a_norm_g=w["a_norm_g"][l].reshape(1, D_A),
            a_ws=w["a_ws"][l], a_bs=w["a_bs"][l],
            b_conv_w=w["b_conv_w"][l],
            alog=_pad_lanes(w["b_a_log"][l]), dtb=_pad_lanes(w["b_dt_bias"][l]),
            b_norm_g=w["b_norm_g"][l][None],
            lamv=jnp.stack([w["c_lam_q1"][l], w["c_lam_k1"][l], w["c_lam_q2"][l], w["c_lam_k2"][l]]),
            c_norm_g=w["c_norm_g"][l][None],
            w_out=w["w_out"][l].astype(bf16),
            norm2_g=w["norm2_g"][l][None],
            w_up=w["w_up"][l].astype(bf16),
            ffn_conv_w=w["ffn_conv_w"][l], ffn_conv_b=w["ffn_conv_b"][l][None],
            w_down=w["w_down"][l].astype(bf16),
        ))
    return out


def _run_trunk(x, lw, final_g, nseq, seq_rows, t_valid, past):
    m = x.shape[0]
    sample = past is not None
    tm = min(512, m)
    k_rows, v_rows, deltas, convs, ffns, chunk_vs = [], [], [], [], [], []
    for l in range(DEPTH):
        w = lw[l]
        lam_init = 0.8 - 0.6 * math.exp(-0.3 * l)
        p, pab = _in_proj(x, w["norm1_g"], w["w_main"], w["w_ab"], tm, N_MAIN // 5)
        if sample:
            r = m
            a_w = jnp.tile(w["a_ws"][:, :seq_rows, :seq_rows], (1, nseq, nseq))
            a_b = jnp.tile(w["a_bs"][:, :seq_rows].T, (nseq, 1))
            oa, va = _chunk_mlp(p, w["a_norm_g"], a_w, jnp.pad(a_b, ((0, 0), (0, LANES - H_A))),
                                r, seq_rows, m, f32)
            ob, s_new, conv_new = _delta(
                p, pab, past["state_conv_qkv"][:, l], past["state_delta"][:, l], w["b_conv_w"], w["alog"],
                w["dtb"], w["b_norm_g"], nseq, seq_rows, t_valid, 1, f32)
            oc = _paged_attn(p, w["lamv"], w["c_norm_g"], past["cache_k"], past["cache_v"],
                             past["page_table"], l, 8, lam_init)
            ffn_hist = past["state_ffn_conv"][:, l]
            nseq_blk = nseq
        else:
            a_b = jnp.pad(w["a_bs"].T, ((0, 0), (0, LANES - H_A)))
            oa, va = _chunk_mlp(p, w["a_norm_g"], w["a_ws"], a_b, CHUNK_A, CHUNK_A, min(512, m), bf16)
            nchunks = seq_rows // CHUNK_B
            ob, s_new, conv_new = _delta(
                p, pab, jnp.zeros((nseq, CONV_B - 1, 3 * D_B), f32),
                jnp.zeros((nseq, H_B, HEAD_DIM, HEAD_DIM), f32), w["b_conv_w"], w["alog"], w["dtb"],
                w["b_norm_g"], nseq, CHUNK_B, CHUNK_B, nchunks, bf16)
            oc = _attn(p, w["lamv"], w["c_norm_g"], nseq, seq_rows, min(512, seq_rows), lam_init)
            ffn_hist = jnp.zeros((nseq, CONV_FFN - 1, 2 * D_FF), f32)
            nseq_blk = 1
        x, h2 = _out_proj(oa, ob, oc, x, w["w_out"], w["norm2_g"], tm)
        act, hg, hv = _ffn_up(h2, w["w_up"], w["ffn_conv_w"], w["ffn_conv_b"], ffn_hist,
                              nseq_blk, seq_rows, t_valid, 256)
        x = _ffn_down(act, w["w_down"], x, final_g[None], tm, 1408, l == DEPTH - 1)
        p3 = p.reshape(nseq, seq_rows, N_MAIN)[:, :t_valid]
        k_rows.append(p3[:, :, P_CK:P_CK + D_C].reshape(nseq, t_valid, H_C, HEAD_DIM))
        v_rows.append(p3[:, :, P_CV:P_CV + D_C].reshape(nseq, t_valid, H_C, HEAD_DIM))
        deltas.append(s_new)
        convs.append(conv_new)
        ffns.append(jnp.concatenate([hg, hv], axis=-1))
        if sample:
            chunk_vs.append(va.reshape(nseq, seq_rows, H_A, HEAD_DIM)[:, :t_valid])
    y = x.reshape(nseq, seq_rows, D_MODEL)[:, :t_valid]
    stack = lambda xs: jnp.stack(xs, axis=1)
    return (y, stack(k_rows), stack(v_rows), stack(deltas), stack(convs), stack(ffns),
            stack(chunk_vs) if sample else None)


def kernel(x_prompt, x_sample, cache_k, cache_v, page_table, state_delta, state_conv_qkv, state_ffn_conv,
           norm1_g, w_in, a_norm_g, a_ws, a_bs, b_conv_w, b_a_log, b_dt_bias, b_norm_g, c_lam_q1, c_lam_k1,
           c_lam_q2, c_lam_k2, c_norm_g, w_out, norm2_g, w_up, ffn_conv_w, ffn_conv_b, w_down, final_g):
    weights = dict(norm1_g=norm1_g, w_in=w_in, a_norm_g=a_norm_g, a_ws=a_ws, a_bs=a_bs, b_conv_w=b_conv_w,
                   b_a_log=b_a_log, b_dt_bias=b_dt_bias, b_norm_g=b_norm_g, c_lam_q1=c_lam_q1,
                   c_lam_k1=c_lam_k1, c_lam_q2=c_lam_q2, c_lam_k2=c_lam_k2, c_norm_g=c_norm_g, w_out=w_out,
                   norm2_g=norm2_g, w_up=w_up, ffn_conv_w=ffn_conv_w, ffn_conv_b=ffn_conv_b, w_down=w_down)
    lw = _prep_weights(weights)

    batch, seq, _ = x_prompt.shape
    yp, pk, pv, pd, pc, pf, _ = _run_trunk(
        x_prompt.reshape(batch * seq, D_MODEL), lw, final_g, batch, seq, seq, None)

    dbatch, dseq, _ = x_sample.shape
    xs = jnp.pad(x_sample, ((0, 0), (0, SAMPLE_ROWS - dseq), (0, 0))).reshape(dbatch * SAMPLE_ROWS, D_MODEL)
    past = dict(
        cache_k=jnp.transpose(cache_k, (0, 1, 3, 2, 4)),
        cache_v=jnp.transpose(cache_v, (0, 1, 3, 2, 4)),
        page_table=page_table.reshape(-1).astype(jnp.int32),
        state_delta=state_delta, state_conv_qkv=state_conv_qkv, state_ffn_conv=state_ffn_conv)
    ys, sk, sv, sd, sc, sf, scv = _run_trunk(xs, lw, final_g, dbatch, SAMPLE_ROWS, dseq, past)
    return (yp, ys, pk, pv, pd, pc, pf, sk, sv, sd, sc, sf, scv)
```

```python
import functools
import math

import jax
import jax.numpy as jnp
from jax import lax
from jax.experimental import pallas as pl
from jax.experimental.pallas import tpu as pltpu

f32 = jnp.float32
bf16 = jnp.bfloat16

D_MODEL = 2048
DEPTH = 4
HEAD_DIM = 128
H_A = 4
D_A = H_A * HEAD_DIM
CHUNK_A = 128
H_B = 6
D_B = H_B * HEAD_DIM
H_C = 6
D_C = H_C * HEAD_DIM
DC_HALF = HEAD_DIM // 2
CONV_B = 4
CHUNK_B = 64
D_FF = 5632
CONV_FFN = 3
EPS = 1e-6
PAGE_SIZE = 128
SAMPLE_ROWS = 8

OFF_AU = 0
OFF_AV = OFF_AU + D_A
OFF_BQKV = OFF_AV + D_A
OFF_BZ = OFF_BQKV + 3 * D_B
OFF_BA = OFF_BZ + D_B
OFF_BB = OFF_BA + H_B
OFF_CQ = OFF_BB + H_B
OFF_CK = OFF_CQ + D_C
OFF_CV = OFF_CK + D_C
N_IN = OFF_CV + D_C

P_BQKV = 0
P_BZ = P_BQKV + 3 * D_B
P_CQ = P_BZ + D_B
P_CK = P_CQ + D_C
P_CV = P_CK + D_C
P_AU = P_CV + D_C
P_AV = P_AU + D_A
N_MAIN = P_AV + D_A
LANES = 128

MXU_WIDTH = 256
FFN_SUB = MXU_WIDTH
ACC_ROWS = 512
FFN_ROWS = ACC_ROWS
VMEM_LIMIT = 56 * 1024 * 1024
NEG_BIG = -1e30
HI = lax.Precision.HIGHEST
NT = (((1,), (1,)), ((), ()))
TN = (((0,), (0,)), ((), ()))


def _params(*sem):
    return pltpu.CompilerParams(dimension_semantics=sem, vmem_limit_bytes=VMEM_LIMIT)


def _sigmoid(x):
    return 1.0 / (1.0 + jnp.exp(-x))


def _gelu_tanh(x):
    c = math.sqrt(2.0 / math.pi)
    return 0.5 * x * (1.0 + jnp.tanh(c * (x + 0.044715 * (x * x * x))))


def _rms(x, g):
    return x * lax.rsqrt(jnp.mean(x * x, axis=-1, keepdims=True) + EPS) * g


def _in_proj_kernel(x_ref, g_ref, w_ref, wab_ref, p_ref, pab_ref, h_scr):
    @pl.when(pl.program_id(1) == 0)
    def _():
        h = _rms(x_ref[...], g_ref[...]).astype(bf16)
        h_scr[...] = h
        pab_ref[...] = lax.dot_general(h, wab_ref[...], NT, preferred_element_type=f32)

    tm, tn = p_ref.shape
    rb = min(tm, ACC_ROWS)
    for n in range(tn // MXU_WIDTH):
        cols = slice(n * MXU_WIDTH, (n + 1) * MXU_WIDTH)
        for r in range(0, tm, rb):
            p_ref[r:r + rb, cols] = lax.dot_general(h_scr[r:r + rb, :], w_ref[cols, :], NT,
                                                    preferred_element_type=f32)


def _in_proj(x, g, w, wab, tm, tn):
    m = x.shape[0]
    n = w.shape[0]
    return pl.pallas_call(
        _in_proj_kernel,
        grid=(m // tm, n // tn),
        in_specs=[
            pl.BlockSpec((tm, D_MODEL), lambda i, j: (i, 0)),
            pl.BlockSpec((1, D_MODEL), lambda i, j: (0, 0)),
            pl.BlockSpec((tn, D_MODEL), lambda i, j: (j, 0)),
            pl.BlockSpec((LANES, D_MODEL), lambda i, j: (0, 0)),
        ],
        out_specs=[
            pl.BlockSpec((tm, tn), lambda i, j: (i, j)),
            pl.BlockSpec((tm, LANES), lambda i, j: (i, 0)),
        ],
        out_shape=[jax.ShapeDtypeStruct((m, n), f32), jax.ShapeDtypeStruct((m, LANES), f32)],
        scratch_shapes=[pltpu.VMEM((tm, D_MODEL), bf16)],
        compiler_params=_params("parallel", "arbitrary"),
        name="in_proj",
    )(x, g, w, wab)


def _chunk_mlp_kernel(u0_ref, u1_ref, v0_ref, v1_ref, g_ref, w_ref, bs_ref, oa_ref, va_ref, *, r, seq_shift):
    nsub = u0_ref.shape[0] // r
    row = lax.broadcasted_iota(jnp.int32, (r, r), 0)
    col = lax.broadcasted_iota(jnp.int32, (r, r), 1)
    same_seq = lax.shift_right_logical(row, seq_shift) == lax.shift_right_logical(col, seq_shift)
    mask = jnp.logical_and(same_seq, col <= row)
    g = g_ref[...]
    for h in range(H_A):
        wm = jnp.where(mask, w_ref[h], 0.0).astype(bf16)
        u_ref = (u0_ref, u1_ref)[h // 2]
        v_ref = (v0_ref, v1_ref)[h // 2]
        lo = (h % 2) * HEAD_DIM
        for c in range(nsub):
            rows = slice(c * r, (c + 1) * r)
            u = _gelu_tanh(u_ref[rows, lo:lo + HEAD_DIM])
            v = _rms(_gelu_tanh(v_ref[rows, lo:lo + HEAD_DIM]), g[:, h * HEAD_DIM:(h + 1) * HEAD_DIM])
            va_ref[rows, h * HEAD_DIM:(h + 1) * HEAD_DIM] = v
            mixed = jnp.dot(wm, v.astype(bf16), preferred_element_type=f32) + bs_ref[:, h:h + 1]
            oa_ref[rows, h * HEAD_DIM:(h + 1) * HEAD_DIM] = (u * mixed).astype(oa_ref.dtype)


def _chunk_mlp(p, g, w, bs_rows, r, seq_rows, rb, out_dtype):
    m = p.shape[0]
    half = 2 * HEAD_DIM
    cu, cv = P_AU // half, P_AV // half
    kern = functools.partial(_chunk_mlp_kernel, r=r, seq_shift=int(math.log2(seq_rows)))
    return pl.pallas_call(
        kern,
        grid=(m // rb,),
        in_specs=[
            pl.BlockSpec((rb, half), lambda i: (i, cu)),
            pl.BlockSpec((rb, half), lambda i: (i, cu + 1)),
            pl.BlockSpec((rb, half), lambda i: (i, cv)),
            pl.BlockSpec((rb, half), lambda i: (i, cv + 1)),
            pl.BlockSpec((1, D_A), lambda i: (0, 0)),
            pl.BlockSpec((H_A, r, r), lambda i: (0, 0, 0)),
            pl.BlockSpec((r, LANES), lambda i: (0, 0)),
        ],
        out_specs=[
            pl.BlockSpec((rb, D_A), lambda i: (i, 0)),
            pl.BlockSpec((rb, D_A), lambda i: (i, 0)),
        ],
        out_shape=[jax.ShapeDtypeStruct((m, D_A), out_dtype), jax.ShapeDtypeStruct((m, D_A), f32)],
        compiler_params=_params("parallel"),
        name="chunk_mlp",
    )(p, p, p, p, g, w, bs_rows)


def _softplus(x):
    return jnp.maximum(x, 0.0) + jnp.log(1.0 + jnp.exp(-jnp.abs(x)))


def _mm(a, b):
    return jnp.dot(a.astype(bf16), b.astype(bf16), preferred_element_type=f32)


def _delta_kernel(x_ref, z_ref, ab_ref, hist_ref, s0_ref, cw_ref, alog_ref, dtb_ref, ng_ref,
                  o_ref, s_ref, cn_ref, ext, *, rows, t_valid, nchunks):
    c = CHUNK_B
    ci = pl.program_id(1)

    @pl.when(ci == 0)
    def _():
        ext[0:8, :] = jnp.zeros((8, 3 * D_B), f32)
        ext[5:8, :] = hist_ref[...]
        s_ref[...] = s0_ref[...]

    x = x_ref[...]
    ext[8:8 + rows, :] = x
    cw = cw_ref[...]
    y = (ext[5:5 + rows, :] * cw[0:1] + ext[6:6 + rows, :] * cw[1:2]
         + ext[7:7 + rows, :] * cw[2:3] + x * cw[3:4])

    @pl.when(ci == nchunks - 1)
    def _():
        cn_ref[...] = ext[5 + t_valid:8 + t_valid, :]

    ext[0:8, :] = ext[rows:rows + 8, :]

    y = y * _sigmoid(y)
    z = z_ref[...]
    ab = ab_ref[...]
    if rows < c:
        y = jnp.concatenate([y, jnp.zeros((c - rows, y.shape[1]), f32)], axis=0)
        z = jnp.concatenate([z, jnp.zeros((c - rows, z.shape[1]), f32)], axis=0)
        ab = jnp.concatenate([ab, jnp.zeros((c - rows, ab.shape[1]), f32)], axis=0)
    masked = t_valid < c
    rmask = lax.broadcasted_iota(jnp.int32, (c, 1), 0) < t_valid

    g_all = -jnp.exp(alog_ref[...]) * _softplus(ab + dtb_ref[...])
    beta_all = _sigmoid(ab)
    if masked:
        g_all = jnp.where(rmask, g_all, 0.0)
        beta_all = jnp.where(rmask, beta_all, 0.0)

    row = lax.broadcasted_iota(jnp.int32, (c, c), 0)
    col = lax.broadcasted_iota(jnp.int32, (c, c), 1)
    incl = row >= col
    strict = row > col
    eye = (row == col).astype(f32)
    gc_all = jnp.dot(incl.astype(f32), g_all, precision=HI, preferred_element_type=f32)
    egc_all = jnp.exp(gc_all)
    blk8 = lax.shift_right_logical(row, 3) == lax.shift_right_logical(col, 3)
    lvl_masks = []
    for sh in range(3, int(math.log2(c))):
        rb = lax.shift_right_logical(row, sh)
        cb = lax.shift_right_logical(col, sh)
        lvl_masks.append(jnp.logical_and(jnp.bitwise_and(rb, 1) == 1, cb == rb - 1))
    gc_t = jnp.transpose(jnp.concatenate([gc_all, jnp.zeros((LANES - c, LANES), f32)], axis=0))
    ng = ng_ref[...]

    hs = range(H_B)
    hd = lambda h, base=0: slice(base + h * HEAD_DIM, base + (h + 1) * HEAD_DIM)

    q = [y[:, hd(h)] for h in hs]
    k = [y[:, hd(h, D_B)] for h in hs]
    v = [y[:, hd(h, 2 * D_B)] for h in hs]
    q = [t * lax.rsqrt(jnp.sum(t * t, axis=-1, keepdims=True) + EPS) * (HEAD_DIM ** -0.5) for t in q]
    k = [t * lax.rsqrt(jnp.sum(t * t, axis=-1, keepdims=True) + EPS) for t in k]
    if masked:
        q = [jnp.where(rmask, t, 0.0) for t in q]
        k = [jnp.where(rmask, t, 0.0) for t in k]
        v = [jnp.where(rmask, t, 0.0) for t in v]
    gc = [gc_all[:, h:h + 1] for h in hs]
    egc = [egc_all[:, h:h + 1] for h in hs]
    beta = [beta_all[:, H_B + h:H_B + h + 1] for h in hs]
    decay = [jnp.exp(jnp.where(incl, gc[h] - gc_t[h:h + 1, 0:c], NEG_BIG)) for h in hs]
    kb = [k[h] * beta[h] for h in hs]
    vb = [v[h] * beta[h] for h in hs]
    k16 = [t.astype(bf16) for t in k]
    aq = [lax.dot_general(jnp.concatenate([kb[h], q[h]], axis=0).astype(bf16), k16[h], NT,
                          preferred_element_type=f32) for h in hs]
    a = [aq[h][:c] * jnp.where(strict, decay[h], 0.0) for h in hs]
    qk = [aq[h][c:] * decay[h] for h in hs]
    x = [jnp.where(blk8, -t, 0.0) for t in a]
    tinv = [eye + t for t in x]
    p1 = [_mm(t, t) for t in x]
    pt = [_mm(jnp.concatenate([p1[h], tinv[h]], axis=0), p1[h]) for h in hs]
    tinv = [tinv[h] + pt[h][c:] for h in hs]
    t2 = [_mm(tinv[h], pt[h][:c]) for h in hs]
    tinv = [tinv[h] + t2[h] for h in hs]
    for msk in lvl_masks:
        lt = [_mm(jnp.where(msk, a[h], 0.0), tinv[h]) for h in hs]
        tl = [_mm(tinv[h], lt[h]) for h in hs]
        tinv = [tinv[h] - tl[h] for h in hs]
    rhs = [jnp.concatenate([vb[h], kb[h] * egc[h]], axis=1) for h in hs]
    sol = [rhs[h] + _mm(tinv[h] - eye, rhs[h]) for h in hs]
    s = [s_ref[h] for h in hs]
    wq = [_mm(jnp.concatenate([sol[h][:, HEAD_DIM:], q[h] * egc[h]], axis=0), s[h]) for h in hs]
    vn16 = [(sol[h][:, :HEAD_DIM] - wq[h][:c]).astype(bf16) for h in hs]
    o = [wq[h][c:] + jnp.dot(qk[h].astype(bf16), vn16[h], preferred_element_type=f32) for h in hs]
    g_last = [gc_all[c - 1:c, h:h + 1] for h in hs]
    kd = [(k[h] * jnp.exp(g_last[h] - gc[h])).astype(bf16) for h in hs]
    ds = [lax.dot_general(kd[h], vn16[h], TN, preferred_element_type=f32) for h in hs]
    for h in hs:
        s_ref[h] = s[h] * jnp.exp(g_last[h]) + ds[h]
    for h in hs:
        zz = z[:, hd(h)]
        oh = _rms(o[h], ng) * (zz * _sigmoid(zz))
        o_ref[:, hd(h)] = oh[:rows].astype(o_ref.dtype)


def _delta(p, pab, hist, s0, cw, alog, dtb, ng, nseq, rows, t_valid, nchunks, out_dtype):
    m = p.shape[0]
    kern = functools.partial(_delta_kernel, rows=rows, t_valid=t_valid, nchunks=nchunks)
    return pl.pallas_call(
        kern,
        grid=(nseq, nchunks),
        in_specs=[
            pl.BlockSpec((rows, 3 * D_B), lambda b, c: (b * nchunks + c, P_BQKV // (3 * D_B))),
            pl.BlockSpec((rows, D_B), lambda b, c: (b * nchunks + c, P_BZ // D_B)),
            pl.BlockSpec((rows, LANES), lambda b, c: (b * nchunks + c, 0)),
            pl.BlockSpec((None, CONV_B - 1, 3 * D_B), lambda b, c: (b, 0, 0)),
            pl.BlockSpec((None, H_B, HEAD_DIM, HEAD_DIM), lambda b, c: (b, 0, 0, 0)),
            pl.BlockSpec((CONV_B, 3 * D_B), lambda b, c: (0, 0)),
            pl.BlockSpec((1, LANES), lambda b, c: (0, 0)),
            pl.BlockSpec((1, LANES), lambda b, c: (0, 0)),
            pl.BlockSpec((1, HEAD_DIM), lambda b, c: (0, 0)),
        ],
        out_specs=[
            pl.BlockSpec((rows, D_B), lambda b, c: (b * nchunks + c, 0)),
            pl.BlockSpec((None, H_B, HEAD_DIM, HEAD_DIM), lambda b, c: (b, 0, 0, 0)),
            pl.BlockSpec((None, CONV_B - 1, 3 * D_B), lambda b, c: (b, 0, 0)),
        ],
        out_shape=[
            jax.ShapeDtypeStruct((m, D_B), out_dtype),
            jax.ShapeDtypeStruct((nseq, H_B, HEAD_DIM, HEAD_DIM), f32),
            jax.ShapeDtypeStruct((nseq, CONV_B - 1, 3 * D_B), f32),
        ],
        scratch_shapes=[pltpu.VMEM((rows + 8, 3 * D_B), f32)],
        compiler_params=_params("parallel", "arbitrary"),
        name="delta",
    )(p, p, pab, hist, s0, cw, alog, dtb, ng)


def _lam(lam_ref, lam_init):
    lv = lam_ref[...]
    a = jnp.sum(lv[0:1] * lv[1:2], axis=-1, keepdims=True)
    b = jnp.sum(lv[2:3] * lv[3:4], axis=-1, keepdims=True)
    return jnp.exp(a) - jnp.exp(b) + lam_init


def _attn_kernel(lam_ref, q_ref, k_ref, v_ref, g_ref, o_ref, qs, m_s, l_s, acc, *, t, lam_init):
    qi = pl.program_id(2)
    q = q_ref[...] * (DC_HALF ** -0.5)
    lane = lax.broadcasted_iota(jnp.int32, q.shape, 1)
    qs[0] = jnp.where(lane < DC_HALF, q, 0.0).astype(bf16)
    qs[1] = jnp.where(lane >= DC_HALF, q, 0.0).astype(bf16)
    m_s[...] = jnp.full(m_s.shape, NEG_BIG, f32)
    l_s[...] = jnp.zeros(l_s.shape, f32)
    acc[...] = jnp.zeros(acc.shape, f32)

    def block(j, diag):
        off = pl.multiple_of(j * t, t)
        k = k_ref[pl.ds(off, t), :].astype(bf16)
        v = v_ref[pl.ds(off, t), :].astype(bf16)
        cs = range(2)
        s = [lax.dot_general(qs[c], k, NT, preferred_element_type=f32) for c in cs]
        if diag:
            row = lax.broadcasted_iota(jnp.int32, (t, t), 0)
            col = lax.broadcasted_iota(jnp.int32, (t, t), 1)
            s = [jnp.where(col <= row, sc, NEG_BIG) for sc in s]
        m_old = [m_s[c] for c in cs]
        m_new = [jnp.maximum(m_old[c], jnp.max(s[c], axis=-1, keepdims=True)) for c in cs]
        alpha = [jnp.exp(m_old[c] - m_new[c]) for c in cs]
        p = [jnp.exp(s[c] - jnp.concatenate([m_new[c]] * (t // HEAD_DIM), axis=1)) for c in cs]
        pv = [jnp.dot(p[c].astype(bf16), v, preferred_element_type=f32) for c in cs]
        for c in cs:
            l_s[c] = alpha[c] * l_s[c] + jnp.sum(p[c], axis=-1, keepdims=True)
            acc[c] = alpha[c] * acc[c] + pv[c]
            m_s[c] = m_new[c]

    def body(j, carry):
        block(j, False)
        return carry

    lax.fori_loop(0, qi, body, 0)
    block(qi, True)

    o = acc[0] / l_s[0] - _lam(lam_ref, lam_init) * (acc[1] / l_s[1])
    o = _rms(o, g_ref[...]) * (1.0 - lam_init)
    o_ref[...] = o.astype(o_ref.dtype)


def _attn(p, lamv, g, nseq, seq, t, lam_init):
    m = p.shape[0]
    nb = seq // t
    cq, ck, cv = P_CQ // HEAD_DIM, P_CK // HEAD_DIM, P_CV // HEAD_DIM
    kern = functools.partial(_attn_kernel, t=t, lam_init=lam_init)
    return pl.pallas_call(
        kern,
        grid=(nseq, H_C, nb),
        in_specs=[
            pl.BlockSpec((4, DC_HALF), lambda b, h, i: (0, 0)),
            pl.BlockSpec((t, HEAD_DIM), lambda b, h, i: (b * nb + i, cq + h)),
            pl.BlockSpec((seq, HEAD_DIM), lambda b, h, i: (b, ck + h)),
            pl.BlockSpec((seq, HEAD_DIM), lambda b, h, i: (b, cv + h)),
            pl.BlockSpec((1, HEAD_DIM), lambda b, h, i: (0, 0)),
        ],
        out_specs=pl.BlockSpec((t, HEAD_DIM), lambda b, h, i: (b * nb + i, h)),
        out_shape=jax.ShapeDtypeStruct((m, D_C), bf16),
        scratch_shapes=[
            pltpu.VMEM((2, t, HEAD_DIM), bf16),
            pltpu.VMEM((2, t, HEAD_DIM), f32),
            pltpu.VMEM((2, t, HEAD_DIM), f32),
            pltpu.VMEM((2, t, HEAD_DIM), f32),
        ],
        compiler_params=_params("parallel", "parallel", "arbitrary"),
        name="attn",
    )(lamv, p, p, p, g)


def _paged_attn_kernel(pt_ref, lam_ref, q_ref, kn_ref, vn_ref, g_ref, *rest, npg, lam_init):
    k_refs = rest[:npg]
    v_refs = rest[npg:2 * npg]
    o_ref = rest[2 * npg]
    qs, kb, vb, m_s, l_s, acc = rest[2 * npg + 1:]
    gi = pl.program_id(1)
    hq = 2 * SAMPLE_ROWS

    @pl.when(gi == 0)
    def _():
        q = q_ref[...] * (DC_HALF ** -0.5)
        lane = lax.broadcasted_iota(jnp.int32, (SAMPLE_ROWS, HEAD_DIM), 1)
        for h in range(H_C):
            qh = q[:, h * HEAD_DIM:(h + 1) * HEAD_DIM]
            qs[h] = jnp.concatenate([jnp.where(lane < DC_HALF, qh, 0.0),
                                     jnp.where(lane >= DC_HALF, qh, 0.0)], axis=0).astype(bf16)
        m_s[...] = jnp.full(m_s.shape, NEG_BIG, f32)
        l_s[...] = jnp.zeros(l_s.shape, f32)
        acc[...] = jnp.zeros(acc.shape, f32)

    def update(s, v_of_head):
        m_old = m_s[...]
        m_new = jnp.maximum(m_old, jnp.max(s, axis=-1, keepdims=True))
        alpha = jnp.exp(m_old - m_new)
        p = jnp.exp(s - jnp.concatenate([m_new] * (s.shape[1] // LANES), axis=1))
        l_s[...] = alpha * l_s[...] + jnp.sum(p, axis=-1, keepdims=True)
        p16 = p.astype(bf16)
        pv = jnp.concatenate(
            [jnp.dot(p16[h * hq:(h + 1) * hq], v_of_head(h), preferred_element_type=f32) for h in range(H_C)],
            axis=0)
        acc[...] = alpha * acc[...] + pv
        m_s[...] = m_new

    for i in range(npg):
        for h in range(H_C):
            kb[h, i * PAGE_SIZE:(i + 1) * PAGE_SIZE, :] = k_refs[i][h].astype(bf16)
            vb[h, i * PAGE_SIZE:(i + 1) * PAGE_SIZE, :] = v_refs[i][h].astype(bf16)
    s_all = jnp.concatenate(
        [lax.dot_general(qs[h], kb[h], NT, preferred_element_type=f32) for h in range(H_C)], axis=0)
    update(s_all, lambda h: vb[h])

    @pl.when(gi == pl.num_programs(1) - 1)
    def _():
        pad = jnp.zeros((LANES - SAMPLE_ROWS, HEAD_DIM), f32)
        kn = kn_ref[...]
        vn = vn_ref[...]
        head = lambda x, h: jnp.concatenate([x[:, h * HEAD_DIM:(h + 1) * HEAD_DIM], pad], axis=0).astype(bf16)
        s = jnp.concatenate(
            [lax.dot_general(qs[h], head(kn, h), NT, preferred_element_type=f32) for h in range(H_C)], axis=0)
        row = lax.broadcasted_iota(jnp.int32, s.shape, 0)
        col = lax.broadcasted_iota(jnp.int32, s.shape, 1)
        s = jnp.where(col <= jnp.bitwise_and(row, SAMPLE_ROWS - 1), s, NEG_BIG)
        update(s, lambda h: head(vn, h))
        o12 = acc[...] / l_s[...]
        lam = _lam(lam_ref, lam_init)
        for h in range(H_C):
            r1 = h * hq
            r2 = r1 + SAMPLE_ROWS
            o = o12[r1:r1 + SAMPLE_ROWS] - lam * o12[r2:r2 + SAMPLE_ROWS]
            o_ref[:, h * HEAD_DIM:(h + 1) * HEAD_DIM] = _rms(o, g_ref[...]) * (1.0 - lam_init)


def _paged_attn(p, lamv, g, cache_k, cache_v, page_table, layer, npg, lam_init):
    m = p.shape[0]
    nseq = m // SAMPLE_ROWS
    n_pages = page_table.shape[0] // nseq
    ngroups = n_pages // npg
    nq = 2 * H_C * SAMPLE_ROWS

    def page_spec(i):
        return pl.BlockSpec((None, None, H_C, PAGE_SIZE, HEAD_DIM),
                            lambda b, gq, pt: (pt[b * n_pages + gq * npg + i], layer, 0, 0, 0))

    row_spec = lambda c: pl.BlockSpec((SAMPLE_ROWS, D_C), lambda b, gq, pt: (b, c))
    kern = functools.partial(_paged_attn_kernel, npg=npg, lam_init=lam_init)
    grid_spec = pltpu.PrefetchScalarGridSpec(
        num_scalar_prefetch=1,
        grid=(nseq, ngroups),
        in_specs=[
            pl.BlockSpec((4, DC_HALF), lambda b, gq, pt: (0, 0)),
            row_spec(P_CQ // D_C), row_spec(P_CK // D_C), row_spec(P_CV // D_C),
            pl.BlockSpec((1, HEAD_DIM), lambda b, gq, pt: (0, 0)),
        ] + [page_spec(i) for i in range(npg)] + [page_spec(i) for i in range(npg)],
        out_specs=pl.BlockSpec((SAMPLE_ROWS, D_C), lambda b, gq, pt: (b, 0)),
        scratch_shapes=[
            pltpu.VMEM((H_C, 2 * SAMPLE_ROWS, HEAD_DIM), bf16),
            pltpu.VMEM((H_C, npg * PAGE_SIZE, HEAD_DIM), bf16),
            pltpu.VMEM((H_C, npg * PAGE_SIZE, HEAD_DIM), bf16),
            pltpu.VMEM((nq, LANES), f32),
            pltpu.VMEM((nq, LANES), f32),
            pltpu.VMEM((nq, HEAD_DIM), f32),
        ],
    )
    return pl.pallas_call(
        kern,
        grid_spec=grid_spec,
        out_shape=jax.ShapeDtypeStruct((m, D_C), f32),
        compiler_params=_params("parallel", "arbitrary"),
        name="paged_attn",
    )(page_table, lamv, p, p, p, g, *([cache_k] * npg), *([cache_v] * npg))


def _out_proj_kernel(a_ref, b_ref, c_ref, x_ref, w_ref, g_ref, xo_ref, h_ref):
    mix = jnp.concatenate([a_ref[...].astype(bf16), b_ref[...].astype(bf16), c_ref[...].astype(bf16)], axis=1)
    for n in range(D_MODEL // MXU_WIDTH):
        cols = slice(n * MXU_WIDTH, (n + 1) * MXU_WIDTH)
        xo_ref[:, cols] = x_ref[:, cols] + jnp.dot(mix, w_ref[:, cols], preferred_element_type=f32)
    h_ref[...] = _rms(xo_ref[...], g_ref[...]).astype(bf16)


def _out_proj(a, b, c, x, wo, g, tm):
    m = x.shape[0]
    row = lambda w: pl.BlockSpec((tm, w), lambda i: (i, 0))
    return pl.pallas_call(
        _out_proj_kernel,
        grid=(m // tm,),
        in_specs=[
            row(D_A), row(D_B), row(D_C), row(D_MODEL),
            pl.BlockSpec((D_MODEL, D_MODEL), lambda i: (0, 0), pipeline_mode=pl.Buffered(1)),
            pl.BlockSpec((1, D_MODEL), lambda i: (0, 0)),
        ],
        out_specs=[row(D_MODEL), row(D_MODEL)],
        out_shape=[jax.ShapeDtypeStruct((m, D_MODEL), f32), jax.ShapeDtypeStruct((m, D_MODEL), bf16)],
        compiler_params=_params("parallel"),
        name="out_proj",
    )(a, b, c, x, wo, g)


def _ffn_up_kernel(h_ref, wg_ref, wv_ref, cwg_ref, cwv_ref, cbg_ref, cbv_ref, hg_ref, hv_ref,
                   act_ref, ng_ref, nv_ref, *, nseq, seq_rows, t_valid):
    h = h_ref[...]
    tf = wg_ref.shape[1]
    ts = min(tf, FFN_SUB)
    rin = lax.broadcasted_iota(jnp.int32, (nseq, seq_rows, ts), 1)

    def conv(w_ref, cw_ref, cb_ref, hist_ref, new_ref, cols):
        w = w_ref[:, cols]
        rb = min(h.shape[0], FFN_ROWS)
        up = jnp.concatenate([jnp.dot(h[r:r + rb], w, preferred_element_type=f32)
                              for r in range(0, h.shape[0], rb)], axis=0)
        up3 = up.reshape(nseq, seq_rows, ts)
        r1 = pltpu.roll(up, 1, axis=0).reshape(nseq, seq_rows, ts)
        r2 = pltpu.roll(up, 2, axis=0).reshape(nseq, seq_rows, ts)
        hist = hist_ref[:, :, cols]
        h0 = hist[:, 0:1, :]
        h1 = hist[:, 1:2, :]
        prev1 = jnp.where(rin == 0, h1, r1)
        prev2 = jnp.where(rin == 0, h0, jnp.where(rin == 1, h1, r2))
        cw = cw_ref[:, cols]
        y = prev2 * cw[0:1] + prev1 * cw[1:2] + up3 * cw[2:3] + cb_ref[:, cols]
        new_ref[:, :, cols] = up3[:, t_valid - 2:t_valid, :]
        return y

    for sub in range(tf // ts):
        cols = slice(sub * ts, (sub + 1) * ts)
        yg = conv(wg_ref, cwg_ref, cbg_ref, hg_ref, ng_ref, cols)
        yv = conv(wv_ref, cwv_ref, cbv_ref, hv_ref, nv_ref, cols)
        act = yg * _sigmoid(yg) * yv
        act_ref[:, cols] = act.reshape(nseq * seq_rows, ts).astype(bf16)


def _ffn_up(h2, w_up, cw, cb, hist, nseq_blk, seq_rows, t_valid, tf):
    m = h2.shape[0]
    tm = nseq_blk * seq_rows
    nseq = m // seq_rows
    nj = D_FF // tf
    kern = functools.partial(_ffn_up_kernel, nseq=nseq_blk, seq_rows=seq_rows, t_valid=t_valid)
    gate = lambda i, j: (0, j)
    val = lambda i, j: (0, nj + j)
    return pl.pallas_call(
        kern,
        grid=(m // tm, nj),
        in_specs=[
            pl.BlockSpec((tm, D_MODEL), lambda i, j: (i, 0)),
            pl.BlockSpec((D_MODEL, tf), gate),
            pl.BlockSpec((D_MODEL, tf), val),
            pl.BlockSpec((CONV_FFN, tf), gate),
            pl.BlockSpec((CONV_FFN, tf), val),
            pl.BlockSpec((1, tf), gate),
            pl.BlockSpec((1, tf), val),
            pl.BlockSpec((nseq_blk, CONV_FFN - 1, tf), lambda i, j: (i, 0, j)),
            pl.BlockSpec((nseq_blk, CONV_FFN - 1, tf), lambda i, j: (i, 0, nj + j)),
        ],
        out_specs=[
            pl.BlockSpec((tm, tf), lambda i, j: (i, j)),
            pl.BlockSpec((nseq_blk, CONV_FFN - 1, tf), lambda i, j: (i, 0, j)),
            pl.BlockSpec((nseq_blk, CONV_FFN - 1, tf), lambda i, j: (i, 0, j)),
        ],
        out_shape=[
            jax.ShapeDtypeStruct((m, D_FF), bf16),
            jax.ShapeDtypeStruct((nseq, CONV_FFN - 1, D_FF), f32),
            jax.ShapeDtypeStruct((nseq, CONV_FFN - 1, D_FF), f32),
        ],
        compiler_params=_params("parallel", "arbitrary"),
        name="ffn_up",
    )(h2, w_up, w_up, cw, cw, cb, cb, hist, hist)


def _ffn_down_kernel(a_ref, w_ref, x_ref, g_ref, o_ref, *, final_norm):
    a = a_ref[...]
    for n in range(D_MODEL // MXU_WIDTH):
        cols = slice(n * MXU_WIDTH, (n + 1) * MXU_WIDTH)
        o_ref[:, cols] = x_ref[:, cols] + jnp.dot(a, w_ref[:, cols], preferred_element_type=f32)
    if final_norm:
        o_ref[...] = _rms(o_ref[...], g_ref[...])


def _ffn_down(act, w_down, x, g, tm, final_norm):
    m = x.shape[0]
    kern = functools.partial(_ffn_down_kernel, final_norm=final_norm)
    return pl.pallas_call(
        kern,
        grid=(m // tm,),
        in_specs=[
            pl.BlockSpec((tm, D_FF), lambda i: (i, 0)),
            pl.BlockSpec((D_FF, D_MODEL), lambda i: (0, 0), pipeline_mode=pl.Buffered(1)),
            pl.BlockSpec((tm, D_MODEL), lambda i: (i, 0)),
            pl.BlockSpec((1, D_MODEL), lambda i: (0, 0)),
        ],
        out_specs=pl.BlockSpec((tm, D_MODEL), lambda i: (i, 0)),
        out_shape=jax.ShapeDtypeStruct((m, D_MODEL), f32),
        compiler_params=_params("parallel"),
        name="ffn_down",
    )(act, w_down, x, g)


def _pad_lanes(v, offset=0):
    return jnp.zeros((1, LANES), f32).at[0, offset:offset + v.shape[0]].set(v)


def _prep_weights(w):
    out = []
    w_in_t = jnp.swapaxes(w["w_in"], 1, 2)
    for l in range(DEPTH):
        wi = w_in_t[l]
        w_main = jnp.concatenate([wi[OFF_BQKV:OFF_BA], wi[OFF_CQ:N_IN], wi[OFF_AU:OFF_BQKV]], axis=0).astype(bf16)
        w_ab = jnp.pad(wi[OFF_BA:OFF_CQ], ((0, LANES - 2 * H_B), (0, 0))).astype(bf16)
        out.append(dict(
            norm1_g=w["norm1_g"][l][None],
            w_main=w_main, w_ab=w_ab,
            a_norm_g=w["a_norm_g"][l].reshape(1, D_A),
            a_ws=w["a_ws"][l], a_bs=w["a_bs"][l],
            b_conv_w=w["b_conv_w"][l],
            alog=_pad_lanes(w["b_a_log"][l]), dtb=_pad_lanes(w["b_dt_bias"][l]),
            b_norm_g=w["b_norm_g"][l][None],
            lamv=jnp.stack([w["c_lam_q1"][l], w["c_lam_k1"][l], w["c_lam_q2"][l], w["c_lam_k2"][l]]),
            c_norm_g=w["c_norm_g"][l][None],
            w_out=w["w_out"][l].astype(bf16),
            norm2_g=w["norm2_g"][l][None],
            w_up=w["w_up"][l].astype(bf16),
            ffn_conv_w=w["ffn_conv_w"][l], ffn_conv_b=w["ffn_conv_b"][l][None],
            w_down=w["w_down"][l].astype(bf16),
        ))
    return out


def _run_trunk(x, lw, final_g, nseq, seq_rows, t_valid, past):
    m = x.shape[0]
    sample = past is not None
    tm = min(512, m)
    k_rows, v_rows, deltas, convs, ffns, chunk_vs = [], [], [], [], [], []
    for l in range(DEPTH):
        w = lw[l]
        lam_init = 0.8 - 0.6 * math.exp(-0.3 * l)
        p, pab = _in_proj(x, w["norm1_g"], w["w_main"], w["w_ab"], min(1024, m), N_MAIN // 5)
        if sample:
            r = m
            a_w = jnp.tile(w["a_ws"][:, :seq_rows, :seq_rows], (1, nseq, nseq))
            a_b = jnp.tile(w["a_bs"][:, :seq_rows].T, (nseq, 1))
            oa, va = _chunk_mlp(p, w["a_norm_g"], a_w, jnp.pad(a_b, ((0, 0), (0, LANES - H_A))),
                                r, seq_rows, m, f32)
            ob, s_new, conv_new = _delta(
                p, pab, past["state_conv_qkv"][:, l], past["state_delta"][:, l], w["b_conv_w"], w["alog"],
                w["dtb"], w["b_norm_g"], nseq, seq_rows, t_valid, 1, f32)
            oc = _paged_attn(p, w["lamv"], w["c_norm_g"], past["cache_k"], past["cache_v"],
                             past["page_table"], l, 8, lam_init)
            ffn_hist = past["state_ffn_conv"][:, l]
            nseq_blk = nseq
        else:
            a_b = jnp.pad(w["a_bs"].T, ((0, 0), (0, LANES - H_A)))
            oa, va = _chunk_mlp(p, w["a_norm_g"], w["a_ws"], a_b, CHUNK_A, CHUNK_A, min(512, m), bf16)
            nchunks = seq_rows // CHUNK_B
            ob, s_new, conv_new = _delta(
                p, pab, jnp.zeros((nseq, CONV_B - 1, 3 * D_B), f32),
                jnp.zeros((nseq, H_B, HEAD_DIM, HEAD_DIM), f32), w["b_conv_w"], w["alog"], w["dtb"],
                w["b_norm_g"], nseq, CHUNK_B, CHUNK_B, nchunks, bf16)
            oc = _attn(p, w["lamv"], w["c_norm_g"], nseq, seq_rows, min(512, seq_rows), lam_init)
            ffn_hist = jnp.zeros((nseq, CONV_FFN - 1, 2 * D_FF), f32)
            nseq_blk = 1
        x, h2 = _out_proj(oa, ob, oc, x, w["w_out"], w["norm2_g"], tm)
        act, hg, hv = _ffn_up(h2, w["w_up"], w["ffn_conv_w"], w["ffn_conv_b"], ffn_hist,
                              nseq_blk, seq_rows, t_valid, 512)
        x = _ffn_down(act, w["w_down"], x, final_g[None], tm, l == DEPTH - 1)
        p3 = p.reshape(nseq, seq_rows, N_MAIN)[:, :t_valid]
        k_rows.append(p3[:, :, P_CK:P_CK + D_C].reshape(nseq, t_valid, H_C, HEAD_DIM))
        v_rows.append(p3[:, :, P_CV:P_CV + D_C].reshape(nseq, t_valid, H_C, HEAD_DIM))
        deltas.append(s_new)
        convs.append(conv_new)
        ffns.append(jnp.concatenate([hg, hv], axis=-1))
        if sample:
            chunk_vs.append(va.reshape(nseq, seq_rows, H_A, HEAD_DIM)[:, :t_valid])
    y = x.reshape(nseq, seq_rows, D_MODEL)[:, :t_valid]
    stack = lambda xs: jnp.stack(xs, axis=1)
    return (y, stack(k_rows), stack(v_rows), stack(deltas), stack(convs), stack(ffns),
            stack(chunk_vs) if sample else None)


def kernel(x_prompt, x_sample, cache_k, cache_v, page_table, state_delta, state_conv_qkv, state_ffn_conv,
           norm1_g, w_in, a_norm_g, a_ws, a_bs, b_conv_w, b_a_log, b_dt_bias, b_norm_g, c_lam_q1, c_lam_k1,
           c_lam_q2, c_lam_k2, c_norm_g, w_out, norm2_g, w_up, ffn_conv_w, ffn_conv_b, w_down, final_g):
    weights = dict(norm1_g=norm1_g, w_in=w_in, a_norm_g=a_norm_g, a_ws=a_ws, a_bs=a_bs, b_conv_w=b_conv_w,
                   b_a_log=b_a_log, b_dt_bias=b_dt_bias, b_norm_g=b_norm_g, c_lam_q1=c_lam_q1,
                   c_lam_k1=c_lam_k1, c_lam_q2=c_lam_q2, c_lam_k2=c_lam_k2, c_norm_g=c_norm_g, w_out=w_out,
                   norm2_g=norm2_g, w_up=w_up, ffn_conv_w=ffn_conv_w, ffn_conv_b=ffn_conv_b, w_down=w_down)
    lw = _prep_weights(weights)

    batch, seq, _ = x_prompt.shape
    yp, pk, pv, pd, pc, pf, _ = _run_trunk(
        x_prompt.reshape(batch * seq, D_MODEL), lw, final_g, batch, seq, seq, None)

    dbatch, dseq, _ = x_sample.shape
    xs = jnp.pad(x_sample, ((0, 0), (0, SAMPLE_ROWS - dseq), (0, 0))).reshape(dbatch * SAMPLE_ROWS, D_MODEL)
    past = dict(
        cache_k=jnp.transpose(cache_k, (0, 1, 3, 2, 4)),
        cache_v=jnp.transpose(cache_v, (0, 1, 3, 2, 4)),
        page_table=page_table.reshape(-1).astype(jnp.int32),
        state_delta=state_delta, state_conv_qkv=state_conv_qkv, state_ffn_conv=state_ffn_conv)
    ys, sk, sv, sd, sc, sf, scv = _run_trunk(xs, lw, final_g, dbatch, SAMPLE_ROWS, dseq, past)
    return (yp, ys, pk, pv, pd, pc, pf, sk, sv, sd, sc, sf, scv)
```

```python
import functools
import math

import jax
import jax.numpy as jnp
from jax import lax
from jax.experimental import pallas as pl
from jax.experimental.pallas import tpu as pltpu

f32 = jnp.float32
bf16 = jnp.bfloat16

D_MODEL = 2048
DEPTH = 4
HEAD_DIM = 128
H_A = 4
D_A = H_A * HEAD_DIM
CHUNK_A = 128
H_B = 6
D_B = H_B * HEAD_DIM
H_C = 6
D_C = H_C * HEAD_DIM
DC_HALF = HEAD_DIM // 2
CONV_B = 4
CHUNK_B = 64
D_FF = 5632
CONV_FFN = 3
EPS = 1e-6
PAGE_SIZE = 128
SAMPLE_ROWS = 8
PAGES_PER_STEP = 8

OFF_AU = 0
OFF_AV = OFF_AU + D_A
OFF_BQKV = OFF_AV + D_A
OFF_BZ = OFF_BQKV + 3 * D_B
OFF_BA = OFF_BZ + D_B
OFF_BB = OFF_BA + H_B
OFF_CQ = OFF_BB + H_B
OFF_CK = OFF_CQ + D_C
OFF_CV = OFF_CK + D_C
N_IN = OFF_CV + D_C

P_BQKV = 0
P_BZ = P_BQKV + 3 * D_B
P_CQ = P_BZ + D_B
P_AU = P_CQ + D_C
P_AV = P_AU + D_A
P_AB = P_AV + D_A
N_MAIN = P_AB + 256
N_KV = 2 * D_C
LANES = 128

MXU_WIDTH = 256
FFN_SUB = MXU_WIDTH
ACC_ROWS = 512
FFN_ROWS = ACC_ROWS
VMEM_LIMIT = 56 * 1024 * 1024
NEG_BIG = -1e30
HI = lax.Precision.HIGHEST
NT = (((1,), (1,)), ((), ()))
TN = (((0,), (0,)), ((), ()))


def _params(*sem):
    return pltpu.CompilerParams(dimension_semantics=sem, vmem_limit_bytes=VMEM_LIMIT)


def _sigmoid(x):
    return 1.0 / (1.0 + jnp.exp(-x))


def _gelu_tanh(x):
    c = math.sqrt(2.0 / math.pi)
    return 0.5 * x * (1.0 + jnp.tanh(c * (x + 0.044715 * (x * x * x))))


def _rms(x, g):
    return x * lax.rsqrt(jnp.mean(x * x, axis=-1, keepdims=True) + EPS) * g


def _in_proj_kernel(x_ref, g_ref, w_ref, p_ref, h_ref):
    @pl.when(pl.program_id(1) == 0)
    def _():
        h_ref[...] = _rms(x_ref[...], g_ref[...]).astype(bf16)

    tm, tn = p_ref.shape
    rb = min(tm, ACC_ROWS)
    for n in range(tn // MXU_WIDTH):
        cols = slice(n * MXU_WIDTH, (n + 1) * MXU_WIDTH)
        for r in range(0, tm, rb):
            p_ref[r:r + rb, cols] = lax.dot_general(h_ref[r:r + rb, :], w_ref[cols, :], NT,
                                                    preferred_element_type=f32)


def _in_proj(x, g, w, tm, tn):
    m = x.shape[0]
    n = w.shape[0]
    return pl.pallas_call(
        _in_proj_kernel,
        grid=(m // tm, n // tn),
        in_specs=[
            pl.BlockSpec((tm, D_MODEL), lambda i, j: (i, 0)),
            pl.BlockSpec((1, D_MODEL), lambda i, j: (0, 0)),
            pl.BlockSpec((tn, D_MODEL), lambda i, j: (j, 0)),
        ],
        out_specs=[
            pl.BlockSpec((tm, tn), lambda i, j: (i, j)),
            pl.BlockSpec((tm, D_MODEL), lambda i, j: (i, 0)),
        ],
        out_shape=[jax.ShapeDtypeStruct((m, n), f32), jax.ShapeDtypeStruct((m, D_MODEL), bf16)],
        compiler_params=_params("parallel", "arbitrary"),
        name="in_proj",
    )(x, g, w)


def _kv_proj_kernel(h_ref, w_ref, kin_ref, vin_ref, k_ref, v_ref):
    del kin_ref, vin_ref
    h = h_ref[...]
    for n in range(N_KV // MXU_WIDTH):
        res = lax.dot_general(h, w_ref[n * MXU_WIDTH:(n + 1) * MXU_WIDTH, :], NT, preferred_element_type=f32)
        for half in range(MXU_WIDTH // HEAD_DIM):
            head = n * (MXU_WIDTH // HEAD_DIM) + half
            dst = k_ref if head < H_C else v_ref
            piece = res[:, half * HEAD_DIM:(half + 1) * HEAD_DIM]
            if len(dst.shape) == 3:
                dst[head % H_C] = piece
            else:
                rows = dst.shape[2]
                for b in range(dst.shape[0]):
                    dst[b, head % H_C] = piece[b * rows:(b + 1) * rows]


def _kv_proj(h, w_kv, kbuf, vbuf, layer, seq_rows, tm):
    m = h.shape[0]
    if tm <= seq_rows:
        per_seq = seq_rows // tm
        buf_spec = pl.BlockSpec((None, None, H_C, tm, HEAD_DIM),
                                lambda i: (i // per_seq, layer, 0, i % per_seq, 0))
    else:
        buf_spec = pl.BlockSpec((tm // seq_rows, None, H_C, seq_rows, HEAD_DIM), lambda i: (i, layer, 0, 0, 0))
    return pl.pallas_call(
        _kv_proj_kernel,
        grid=(m // tm,),
        in_specs=[
            pl.BlockSpec((tm, D_MODEL), lambda i: (i, 0)),
            pl.BlockSpec((N_KV, D_MODEL), lambda i: (0, 0), pipeline_mode=pl.Buffered(1)),
            pl.BlockSpec(memory_space=pl.ANY),
            pl.BlockSpec(memory_space=pl.ANY),
        ],
        out_specs=[buf_spec, buf_spec],
        out_shape=[jax.ShapeDtypeStruct(kbuf.shape, f32), jax.ShapeDtypeStruct(vbuf.shape, f32)],
        input_output_aliases={2: 0, 3: 1},
        compiler_params=_params("parallel"),
        name="kv_proj",
    )(h, w_kv, kbuf, vbuf)


def _chunk_mlp_kernel(u0_ref, u1_ref, v0_ref, v1_ref, g_ref, w_ref, bs_ref, oa_ref, va_ref, *, r, seq_shift):
    nsub = u0_ref.shape[0] // r
    row = lax.broadcasted_iota(jnp.int32, (r, r), 0)
    col = lax.broadcasted_iota(jnp.int32, (r, r), 1)
    same_seq = lax.shift_right_logical(row, seq_shift) == lax.shift_right_logical(col, seq_shift)
    mask = jnp.logical_and(same_seq, col <= row)
    g = g_ref[...]
    for h in range(H_A):
        wm = jnp.where(mask, w_ref[h], 0.0).astype(bf16)
        u_ref = (u0_ref, u1_ref)[h // 2]
        v_ref = (v0_ref, v1_ref)[h // 2]
        lo = (h % 2) * HEAD_DIM
        for c in range(nsub):
            rows = slice(c * r, (c + 1) * r)
            u = _gelu_tanh(u_ref[rows, lo:lo + HEAD_DIM])
            v = _rms(_gelu_tanh(v_ref[rows, lo:lo + HEAD_DIM]), g[:, h * HEAD_DIM:(h + 1) * HEAD_DIM])
            va_ref[rows, h * HEAD_DIM:(h + 1) * HEAD_DIM] = v
            mixed = jnp.dot(wm, v.astype(bf16), preferred_element_type=f32) + bs_ref[:, h:h + 1]
            oa_ref[rows, h * HEAD_DIM:(h + 1) * HEAD_DIM] = (u * mixed).astype(oa_ref.dtype)


def _chunk_mlp(p, g, w, bs_rows, r, seq_rows, rb, out_dtype):
    m = p.shape[0]
    half = 2 * HEAD_DIM
    cu, cv = P_AU // half, P_AV // half
    kern = functools.partial(_chunk_mlp_kernel, r=r, seq_shift=int(math.log2(seq_rows)))
    return pl.pallas_call(
        kern,
        grid=(m // rb,),
        in_specs=[
            pl.BlockSpec((rb, half), lambda i: (i, cu)),
            pl.BlockSpec((rb, half), lambda i: (i, cu + 1)),
            pl.BlockSpec((rb, half), lambda i: (i, cv)),
            pl.BlockSpec((rb, half), lambda i: (i, cv + 1)),
            pl.BlockSpec((1, D_A), lambda i: (0, 0)),
            pl.BlockSpec((H_A, r, r), lambda i: (0, 0, 0)),
            pl.BlockSpec((r, LANES), lambda i: (0, 0)),
        ],
        out_specs=[
            pl.BlockSpec((rb, D_A), lambda i: (i, 0)),
            pl.BlockSpec((rb, D_A), lambda i: (i, 0)),
        ],
        out_shape=[jax.ShapeDtypeStruct((m, D_A), out_dtype), jax.ShapeDtypeStruct((m, D_A), f32)],
        compiler_params=_params("parallel"),
        name="chunk_mlp",
    )(p, p, p, p, g, w, bs_rows)


def _softplus(x):
    return jnp.maximum(x, 0.0) + jnp.log(1.0 + jnp.exp(-jnp.abs(x)))


def _mm(a, b):
    return jnp.dot(a.astype(bf16), b.astype(bf16), preferred_element_type=f32)


def _delta_kernel(x_ref, z_ref, ab_ref, hist_ref, s0_ref, cw_ref, alog_ref, dtb_ref, ng_ref,
                  o_ref, s_ref, cn_ref, ext, *, rows, t_valid, nchunks):
    c = CHUNK_B
    ci = pl.program_id(1)

    @pl.when(ci == 0)
    def _():
        ext[0:8, :] = jnp.zeros((8, 3 * D_B), f32)
        ext[5:8, :] = hist_ref[...]
        s_ref[...] = s0_ref[...]

    x = x_ref[...]
    ext[8:8 + rows, :] = x
    cw = cw_ref[...]
    y = (ext[5:5 + rows, :] * cw[0:1] + ext[6:6 + rows, :] * cw[1:2]
         + ext[7:7 + rows, :] * cw[2:3] + x * cw[3:4])

    @pl.when(ci == nchunks - 1)
    def _():
        cn_ref[...] = ext[5 + t_valid:8 + t_valid, :]

    ext[0:8, :] = ext[rows:rows + 8, :]

    y = y * _sigmoid(y)
    z = z_ref[...]
    ab = ab_ref[...]
    if rows < c:
        y = jnp.concatenate([y, jnp.zeros((c - rows, y.shape[1]), f32)], axis=0)
        z = jnp.concatenate([z, jnp.zeros((c - rows, z.shape[1]), f32)], axis=0)
        ab = jnp.concatenate([ab, jnp.zeros((c - rows, ab.shape[1]), f32)], axis=0)
    masked = t_valid < c
    rmask = lax.broadcasted_iota(jnp.int32, (c, 1), 0) < t_valid

    g_all = -jnp.exp(alog_ref[...]) * _softplus(ab + dtb_ref[...])
    beta_all = _sigmoid(ab)
    if masked:
        g_all = jnp.where(rmask, g_all, 0.0)
        beta_all = jnp.where(rmask, beta_all, 0.0)

    row = lax.broadcasted_iota(jnp.int32, (c, c), 0)
    col = lax.broadcasted_iota(jnp.int32, (c, c), 1)
    incl = row >= col
    strict = row > col
    eye = (row == col).astype(f32)
    gc_all = jnp.dot(incl.astype(f32), g_all, precision=HI, preferred_element_type=f32)
    egc_all = jnp.exp(gc_all)
    blk8 = lax.shift_right_logical(row, 3) == lax.shift_right_logical(col, 3)
    lvl_masks = []
    for sh in range(3, int(math.log2(c))):
        rb = lax.shift_right_logical(row, sh)
        cb = lax.shift_right_logical(col, sh)
        lvl_masks.append(jnp.logical_and(jnp.bitwise_and(rb, 1) == 1, cb == rb - 1))
    gc_t = jnp.transpose(jnp.concatenate([gc_all, jnp.zeros((LANES - c, LANES), f32)], axis=0))
    ng = ng_ref[...]

    hs = range(H_B)
    hd = lambda h, base=0: slice(base + h * HEAD_DIM, base + (h + 1) * HEAD_DIM)

    q = [y[:, hd(h)] for h in hs]
    k = [y[:, hd(h, D_B)] for h in hs]
    v = [y[:, hd(h, 2 * D_B)] for h in hs]
    q = [t * lax.rsqrt(jnp.sum(t * t, axis=-1, keepdims=True) + EPS) * (HEAD_DIM ** -0.5) for t in q]
    k = [t * lax.rsqrt(jnp.sum(t * t, axis=-1, keepdims=True) + EPS) for t in k]
    if masked:
        q = [jnp.where(rmask, t, 0.0) for t in q]
        k = [jnp.where(rmask, t, 0.0) for t in k]
        v = [jnp.where(rmask, t, 0.0) for t in v]
    gc = [gc_all[:, h:h + 1] for h in hs]
    egc = [egc_all[:, h:h + 1] for h in hs]
    beta = [beta_all[:, H_B + h:H_B + h + 1] for h in hs]
    decay = [jnp.exp(jnp.where(incl, gc[h] - gc_t[h:h + 1, 0:c], NEG_BIG)) for h in hs]
    kb = [k[h] * beta[h] for h in hs]
    vb = [v[h] * beta[h] for h in hs]
    k16 = [t.astype(bf16) for t in k]
    aq = [lax.dot_general(jnp.concatenate([kb[h], q[h]], axis=0).astype(bf16), k16[h], NT,
                          preferred_element_type=f32) for h in hs]
    a = [aq[h][:c] * jnp.where(strict, decay[h], 0.0) for h in hs]
    qk = [aq[h][c:] * decay[h] for h in hs]
    x = [jnp.where(blk8, -t, 0.0) for t in a]
    tinv = [eye + t for t in x]
    p1 = [_mm(t, t) for t in x]
    pt = [_mm(jnp.concatenate([p1[h], tinv[h]], axis=0), p1[h]) for h in hs]
    tinv = [tinv[h] + pt[h][c:] for h in hs]
    t2 = [_mm(tinv[h], pt[h][:c]) for h in hs]
    tinv = [tinv[h] + t2[h] for h in hs]
    for msk in lvl_masks:
        lt = [_mm(jnp.where(msk, a[h], 0.0), tinv[h]) for h in hs]
        tl = [_mm(tinv[h], lt[h]) for h in hs]
        tinv = [tinv[h] - tl[h] for h in hs]
    rhs = [jnp.concatenate([vb[h], kb[h] * egc[h]], axis=1) for h in hs]
    sol = [rhs[h] + _mm(tinv[h] - eye, rhs[h]) for h in hs]
    s = [s_ref[h] for h in hs]
    wq = [_mm(jnp.concatenate([sol[h][:, HEAD_DIM:], q[h] * egc[h]], axis=0), s[h]) for h in hs]
    vn16 = [(sol[h][:, :HEAD_DIM] - wq[h][:c]).astype(bf16) for h in hs]
    o = [wq[h][c:] + jnp.dot(qk[h].astype(bf16), vn16[h], preferred_element_type=f32) for h in hs]
    g_last = [gc_all[c - 1:c, h:h + 1] for h in hs]
    kd = [(k[h] * jnp.exp(g_last[h] - gc[h])).astype(bf16) for h in hs]
    ds = [lax.dot_general(kd[h], vn16[h], TN, preferred_element_type=f32) for h in hs]
    for h in hs:
        s_ref[h] = s[h] * jnp.exp(g_last[h]) + ds[h]
    for h in hs:
        zz = z[:, hd(h)]
        oh = _rms(o[h], ng) * (zz * _sigmoid(zz))
        o_ref[:, hd(h)] = oh[:rows].astype(o_ref.dtype)


def _delta(p, hist, s0, cw, alog, dtb, ng, nseq, rows, t_valid, nchunks, out_dtype):
    m = p.shape[0]
    kern = functools.partial(_delta_kernel, rows=rows, t_valid=t_valid, nchunks=nchunks)
    return pl.pallas_call(
        kern,
        grid=(nseq, nchunks),
        in_specs=[
            pl.BlockSpec((rows, 3 * D_B), lambda b, c: (b * nchunks + c, P_BQKV // (3 * D_B))),
            pl.BlockSpec((rows, D_B), lambda b, c: (b * nchunks + c, P_BZ // D_B)),
            pl.BlockSpec((rows, LANES), lambda b, c: (b * nchunks + c, P_AB // LANES)),
            pl.BlockSpec((None, CONV_B - 1, 3 * D_B), lambda b, c: (b, 0, 0)),
            pl.BlockSpec((None, H_B, HEAD_DIM, HEAD_DIM), lambda b, c: (b, 0, 0, 0)),
            pl.BlockSpec((CONV_B, 3 * D_B), lambda b, c: (0, 0)),
            pl.BlockSpec((1, LANES), lambda b, c: (0, 0)),
            pl.BlockSpec((1, LANES), lambda b, c: (0, 0)),
            pl.BlockSpec((1, HEAD_DIM), lambda b, c: (0, 0)),
        ],
        out_specs=[
            pl.BlockSpec((rows, D_B), lambda b, c: (b * nchunks + c, 0)),
            pl.BlockSpec((None, H_B, HEAD_DIM, HEAD_DIM), lambda b, c: (b, 0, 0, 0)),
            pl.BlockSpec((None, CONV_B - 1, 3 * D_B), lambda b, c: (b, 0, 0)),
        ],
        out_shape=[
            jax.ShapeDtypeStruct((m, D_B), out_dtype),
            jax.ShapeDtypeStruct((nseq, H_B, HEAD_DIM, HEAD_DIM), f32),
            jax.ShapeDtypeStruct((nseq, CONV_B - 1, 3 * D_B), f32),
        ],
        scratch_shapes=[pltpu.VMEM((rows + 8, 3 * D_B), f32)],
        compiler_params=_params("parallel", "arbitrary"),
        name="delta",
    )(p, p, p, hist, s0, cw, alog, dtb, ng)


def _lam(lam_ref, lam_init):
    lv = lam_ref[...]
    a = jnp.sum(lv[0:1] * lv[1:2], axis=-1, keepdims=True)
    b = jnp.sum(lv[2:3] * lv[3:4], axis=-1, keepdims=True)
    return jnp.exp(a) - jnp.exp(b) + lam_init


def _attn_kernel(lam_ref, q_ref, k_ref, v_ref, g_ref, o_ref, qs, m_s, l_s, acc, *, t, lam_init):
    qi = pl.program_id(2)
    q = q_ref[...] * (DC_HALF ** -0.5)
    lane = lax.broadcasted_iota(jnp.int32, q.shape, 1)
    qs[0] = jnp.where(lane < DC_HALF, q, 0.0).astype(bf16)
    qs[1] = jnp.where(lane >= DC_HALF, q, 0.0).astype(bf16)
    m_s[...] = jnp.full(m_s.shape, NEG_BIG, f32)
    l_s[...] = jnp.zeros(l_s.shape, f32)
    acc[...] = jnp.zeros(acc.shape, f32)

    def block(j, diag):
        off = pl.multiple_of(j * t, t)
        k = k_ref[pl.ds(off, t), :].astype(bf16)
        v = v_ref[pl.ds(off, t), :].astype(bf16)
        cs = range(2)
        s = [lax.dot_general(qs[c], k, NT, preferred_element_type=f32) for c in cs]
        if diag:
            row = lax.broadcasted_iota(jnp.int32, (t, t), 0)
            col = lax.broadcasted_iota(jnp.int32, (t, t), 1)
            s = [jnp.where(col <= row, sc, NEG_BIG) for sc in s]
        m_old = [m_s[c] for c in cs]
        m_new = [jnp.maximum(m_old[c], jnp.max(s[c], axis=-1, keepdims=True)) for c in cs]
        alpha = [jnp.exp(m_old[c] - m_new[c]) for c in cs]
        p = [jnp.exp(s[c] - jnp.concatenate([m_new[c]] * (t // HEAD_DIM), axis=1)) for c in cs]
        pv = [jnp.dot(p[c].astype(bf16), v, preferred_element_type=f32) for c in cs]
        for c in cs:
            l_s[c] = alpha[c] * l_s[c] + jnp.sum(p[c], axis=-1, keepdims=True)
            acc[c] = alpha[c] * acc[c] + pv[c]
            m_s[c] = m_new[c]

    def body(j, carry):
        block(j, False)
        return carry

    lax.fori_loop(0, qi, body, 0)
    block(qi, True)

    o = acc[0] / l_s[0] - _lam(lam_ref, lam_init) * (acc[1] / l_s[1])
    o = _rms(o, g_ref[...]) * (1.0 - lam_init)
    o_ref[...] = o.astype(o_ref.dtype)


def _attn(p, kbuf, vbuf, layer, lamv, g, nseq, seq, t, lam_init):
    m = p.shape[0]
    nb = seq // t
    cq = P_CQ // HEAD_DIM
    kern = functools.partial(_attn_kernel, t=t, lam_init=lam_init)
    kv_spec = pl.BlockSpec((None, None, None, seq, HEAD_DIM), lambda b, h, i: (b, layer, h, 0, 0))
    return pl.pallas_call(
        kern,
        grid=(nseq, H_C, nb),
        in_specs=[
            pl.BlockSpec((4, DC_HALF), lambda b, h, i: (0, 0)),
            pl.BlockSpec((t, HEAD_DIM), lambda b, h, i: (b * nb + i, cq + h)),
            kv_spec,
            kv_spec,
            pl.BlockSpec((1, HEAD_DIM), lambda b, h, i: (0, 0)),
        ],
        out_specs=pl.BlockSpec((t, HEAD_DIM), lambda b, h, i: (b * nb + i, h)),
        out_shape=jax.ShapeDtypeStruct((m, D_C), bf16),
        scratch_shapes=[
            pltpu.VMEM((2, t, HEAD_DIM), bf16),
            pltpu.VMEM((2, t, HEAD_DIM), f32),
            pltpu.VMEM((2, t, HEAD_DIM), f32),
            pltpu.VMEM((2, t, HEAD_DIM), f32),
        ],
        compiler_params=_params("parallel", "parallel", "arbitrary"),
        name="attn",
    )(lamv, p, kbuf, vbuf, g)


def _paged_attn_kernel(pt_ref, lam_ref, q_ref, kn_ref, vn_ref, g_ref, *rest, npg, lam_init):
    k_refs = rest[:npg]
    v_refs = rest[npg:2 * npg]
    o_ref = rest[2 * npg]
    qs, kb, vb, m_s, l_s, acc = rest[2 * npg + 1:]
    gi = pl.program_id(1)
    hq = 2 * SAMPLE_ROWS

    @pl.when(gi == 0)
    def _():
        q = q_ref[...] * (DC_HALF ** -0.5)
        lane = lax.broadcasted_iota(jnp.int32, (SAMPLE_ROWS, HEAD_DIM), 1)
        for h in range(H_C):
            qh = q[:, h * HEAD_DIM:(h + 1) * HEAD_DIM]
            qs[h] = jnp.concatenate([jnp.where(lane < DC_HALF, qh, 0.0),
                                     jnp.where(lane >= DC_HALF, qh, 0.0)], axis=0).astype(bf16)
        m_s[...] = jnp.full(m_s.shape, NEG_BIG, f32)
        l_s[...] = jnp.zeros(l_s.shape, f32)
        acc[...] = jnp.zeros(acc.shape, f32)

    def update(s, v_of_head):
        m_old = m_s[...]
        m_new = jnp.maximum(m_old, jnp.max(s, axis=-1, keepdims=True))
        alpha = jnp.exp(m_old - m_new)
        p = jnp.exp(s - jnp.concatenate([m_new] * (s.shape[1] // LANES), axis=1))
        l_s[...] = alpha * l_s[...] + jnp.sum(p, axis=-1, keepdims=True)
        p16 = p.astype(bf16)
        pv = jnp.concatenate(
            [jnp.dot(p16[h * hq:(h + 1) * hq], v_of_head(h), preferred_element_type=f32) for h in range(H_C)],
            axis=0)
        acc[...] = alpha * acc[...] + pv
        m_s[...] = m_new

    for i in range(npg):
        for h in range(H_C):
            kb[h, i * PAGE_SIZE:(i + 1) * PAGE_SIZE, :] = k_refs[i][h].astype(bf16)
            vb[h, i * PAGE_SIZE:(i + 1) * PAGE_SIZE, :] = v_refs[i][h].astype(bf16)
    s_all = jnp.concatenate(
        [lax.dot_general(qs[h], kb[h], NT, preferred_element_type=f32) for h in range(H_C)], axis=0)
    update(s_all, lambda h: vb[h])

    @pl.when(gi == pl.num_programs(1) - 1)
    def _():
        pad = jnp.zeros((LANES - SAMPLE_ROWS, HEAD_DIM), f32)
        head = lambda x_ref, h: jnp.concatenate([x_ref[h], pad], axis=0).astype(bf16)
        s = jnp.concatenate(
            [lax.dot_general(qs[h], head(kn_ref, h), NT, preferred_element_type=f32) for h in range(H_C)], axis=0)
        row = lax.broadcasted_iota(jnp.int32, s.shape, 0)
        col = lax.broadcasted_iota(jnp.int32, s.shape, 1)
        s = jnp.where(col <= jnp.bitwise_and(row, SAMPLE_ROWS - 1), s, NEG_BIG)
        update(s, lambda h: head(vn_ref, h))
        o12 = acc[...] / l_s[...]
        lam = _lam(lam_ref, lam_init)
        for h in range(H_C):
            r1 = h * hq
            r2 = r1 + SAMPLE_ROWS
            o = o12[r1:r1 + SAMPLE_ROWS] - lam * o12[r2:r2 + SAMPLE_ROWS]
            o_ref[:, h * HEAD_DIM:(h + 1) * HEAD_DIM] = _rms(o, g_ref[...]) * (1.0 - lam_init)


def _paged_attn(p, kbuf, vbuf, lamv, g, cache_k, cache_v, page_table, layer, npg, lam_init):
    m = p.shape[0]
    nseq = m // SAMPLE_ROWS
    n_pages = page_table.shape[0] // nseq
    ngroups = n_pages // npg
    nq = 2 * H_C * SAMPLE_ROWS

    def page_spec(i):
        return pl.BlockSpec((None, None, H_C, PAGE_SIZE, HEAD_DIM),
                            lambda b, gq, pt: (pt[b * n_pages + gq * npg + i], layer, 0, 0, 0))

    new_spec = pl.BlockSpec((None, None, H_C, SAMPLE_ROWS, HEAD_DIM), lambda b, gq, pt: (b, layer, 0, 0, 0))
    kern = functools.partial(_paged_attn_kernel, npg=npg, lam_init=lam_init)
    grid_spec = pltpu.PrefetchScalarGridSpec(
        num_scalar_prefetch=1,
        grid=(nseq, ngroups),
        in_specs=[
            pl.BlockSpec((4, DC_HALF), lambda b, gq, pt: (0, 0)),
            pl.BlockSpec((SAMPLE_ROWS, D_C), lambda b, gq, pt: (b, P_CQ // D_C)), new_spec, new_spec,
            pl.BlockSpec((1, HEAD_DIM), lambda b, gq, pt: (0, 0)),
        ] + [page_spec(i) for i in range(npg)] + [page_spec(i) for i in range(npg)],
        out_specs=pl.BlockSpec((SAMPLE_ROWS, D_C), lambda b, gq, pt: (b, 0)),
        scratch_shapes=[
            pltpu.VMEM((H_C, 2 * SAMPLE_ROWS, HEAD_DIM), bf16),
            pltpu.VMEM((H_C, npg * PAGE_SIZE, HEAD_DIM), bf16),
            pltpu.VMEM((H_C, npg * PAGE_SIZE, HEAD_DIM), bf16),
            pltpu.VMEM((nq, LANES), f32),
            pltpu.VMEM((nq, LANES), f32),
            pltpu.VMEM((nq, HEAD_DIM), f32),
        ],
    )
    return pl.pallas_call(
        kern,
        grid_spec=grid_spec,
        out_shape=jax.ShapeDtypeStruct((m, D_C), f32),
        compiler_params=_params("parallel", "arbitrary"),
        name="paged_attn",
    )(page_table, lamv, p, kbuf, vbuf, g, *([cache_k] * npg), *([cache_v] * npg))


def _out_proj_kernel(a_ref, b_ref, c_ref, x_ref, w_ref, g_ref, xo_ref, h_ref):
    mix = jnp.concatenate([a_ref[...].astype(bf16), b_ref[...].astype(bf16), c_ref[...].astype(bf16)], axis=1)
    for n in range(D_MODEL // MXU_WIDTH):
        cols = slice(n * MXU_WIDTH, (n + 1) * MXU_WIDTH)
        xo_ref[:, cols] = x_ref[:, cols] + jnp.dot(mix, w_ref[:, cols], preferred_element_type=f32)
    h_ref[...] = _rms(xo_ref[...], g_ref[...]).astype(bf16)


def _out_proj(a, b, c, x, wo, g, tm):
    m = x.shape[0]
    row = lambda w: pl.BlockSpec((tm, w), lambda i: (i, 0))
    return pl.pallas_call(
        _out_proj_kernel,
        grid=(m // tm,),
        in_specs=[
            row(D_A), row(D_B), row(D_C), row(D_MODEL),
            pl.BlockSpec((D_MODEL, D_MODEL), lambda i: (0, 0), pipeline_mode=pl.Buffered(1)),
            pl.BlockSpec((1, D_MODEL), lambda i: (0, 0)),
        ],
        out_specs=[row(D_MODEL), row(D_MODEL)],
        out_shape=[jax.ShapeDtypeStruct((m, D_MODEL), f32), jax.ShapeDtypeStruct((m, D_MODEL), bf16)],
        compiler_params=_params("parallel"),
        name="out_proj",
    )(a, b, c, x, wo, g)


def _ffn_up_kernel(h_ref, wg_ref, wv_ref, cwg_ref, cwv_ref, cbg_ref, cbv_ref, hg_ref, hv_ref,
                   act_ref, ng_ref, nv_ref, *, nseq, seq_rows, t_valid):
    h = h_ref[...]
    tf = wg_ref.shape[1]
    ts = min(tf, FFN_SUB)
    rin = lax.broadcasted_iota(jnp.int32, (nseq, seq_rows, ts), 1)

    def conv(w_ref, cw_ref, cb_ref, hist_ref, new_ref, cols):
        w = w_ref[:, cols]
        rb = min(h.shape[0], FFN_ROWS)
        up = jnp.concatenate([jnp.dot(h[r:r + rb], w, preferred_element_type=f32)
                              for r in range(0, h.shape[0], rb)], axis=0)
        up3 = up.reshape(nseq, seq_rows, ts)
        r1 = pltpu.roll(up, 1, axis=0).reshape(nseq, seq_rows, ts)
        r2 = pltpu.roll(up, 2, axis=0).reshape(nseq, seq_rows, ts)
        hist = hist_ref[:, :, cols]
        h0 = hist[:, 0:1, :]
        h1 = hist[:, 1:2, :]
        prev1 = jnp.where(rin == 0, h1, r1)
        prev2 = jnp.where(rin == 0, h0, jnp.where(rin == 1, h1, r2))
        cw = cw_ref[:, cols]
        y = prev2 * cw[0:1] + prev1 * cw[1:2] + up3 * cw[2:3] + cb_ref[:, cols]
        new_ref[:, :, cols] = up3[:, t_valid - 2:t_valid, :]
        return y

    for sub in range(tf // ts):
        cols = slice(sub * ts, (sub + 1) * ts)
        yg = conv(wg_ref, cwg_ref, cbg_ref, hg_ref, ng_ref, cols)
        yv = conv(wv_ref, cwv_ref, cbv_ref, hv_ref, nv_ref, cols)
        act = yg * _sigmoid(yg) * yv
        act_ref[:, cols] = act.reshape(nseq * seq_rows, ts).astype(bf16)


def _ffn_up(h2, w_up, cw, cb, hist, nseq_blk, seq_rows, t_valid, tf):
    m = h2.shape[0]
    tm = nseq_blk * seq_rows
    nseq = m // seq_rows
    nj = D_FF // tf
    kern = functools.partial(_ffn_up_kernel, nseq=nseq_blk, seq_rows=seq_rows, t_valid=t_valid)
    gate = lambda i, j: (0, j)
    val = lambda i, j: (0, nj + j)
    return pl.pallas_call(
        kern,
        grid=(m // tm, nj),
        in_specs=[
            pl.BlockSpec((tm, D_MODEL), lambda i, j: (i, 0)),
            pl.BlockSpec((D_MODEL, tf), gate),
            pl.BlockSpec((D_MODEL, tf), val),
            pl.BlockSpec((CONV_FFN, tf), gate),
            pl.BlockSpec((CONV_FFN, tf), val),
            pl.BlockSpec((1, tf), gate),
            pl.BlockSpec((1, tf), val),
            pl.BlockSpec((nseq_blk, CONV_FFN - 1, tf), lambda i, j: (i, 0, j)),
            pl.BlockSpec((nseq_blk, CONV_FFN - 1, tf), lambda i, j: (i, 0, nj + j)),
        ],
        out_specs=[
            pl.BlockSpec((tm, tf), lambda i, j: (i, j)),
            pl.BlockSpec((nseq_blk, CONV_FFN - 1, tf), lambda i, j: (i, 0, j)),
            pl.BlockSpec((nseq_blk, CONV_FFN - 1, tf), lambda i, j: (i, 0, j)),
        ],
        out_shape=[
            jax.ShapeDtypeStruct((m, D_FF), bf16),
            jax.ShapeDtypeStruct((nseq, CONV_FFN - 1, D_FF), f32),
            jax.ShapeDtypeStruct((nseq, CONV_FFN - 1, D_FF), f32),
        ],
        compiler_params=_params("parallel", "arbitrary"),
        name="ffn_up",
    )(h2, w_up, w_up, cw, cw, cb, cb, hist, hist)


def _ffn_down_kernel(a_ref, w_ref, x_ref, g_ref, o_ref, *, final_norm):
    a = a_ref[...]
    for n in range(D_MODEL // MXU_WIDTH):
        cols = slice(n * MXU_WIDTH, (n + 1) * MXU_WIDTH)
        o_ref[:, cols] = x_ref[:, cols] + jnp.dot(a, w_ref[:, cols], preferred_element_type=f32)
    if final_norm:
        o_ref[...] = _rms(o_ref[...], g_ref[...])


def _ffn_down(act, w_down, x, g, tm, final_norm):
    m = x.shape[0]
    kern = functools.partial(_ffn_down_kernel, final_norm=final_norm)
    return pl.pallas_call(
        kern,
        grid=(m // tm,),
        in_specs=[
            pl.BlockSpec((tm, D_FF), lambda i: (i, 0)),
            pl.BlockSpec((D_FF, D_MODEL), lambda i: (0, 0), pipeline_mode=pl.Buffered(1)),
            pl.BlockSpec((tm, D_MODEL), lambda i: (i, 0)),
            pl.BlockSpec((1, D_MODEL), lambda i: (0, 0)),
        ],
        out_specs=pl.BlockSpec((tm, D_MODEL), lambda i: (i, 0)),
        out_shape=jax.ShapeDtypeStruct((m, D_MODEL), f32),
        compiler_params=_params("parallel"),
        name="ffn_down",
    )(act, w_down, x, g)


def _pad_lanes(v, offset=0):
    return jnp.zeros((1, LANES), f32).at[0, offset:offset + v.shape[0]].set(v)


def _prep_weights(w):
    out = []
    w_in_t = jnp.swapaxes(w["w_in"], 1, 2)
    for l in range(DEPTH):
        wi = w_in_t[l]
        w_main = jnp.concatenate(
            [wi[OFF_BQKV:OFF_BA], wi[OFF_CQ:OFF_CK], wi[OFF_AU:OFF_BQKV], wi[OFF_BA:OFF_CQ],
             jnp.zeros((N_MAIN - P_AB - 2 * H_B, D_MODEL), f32)], axis=0).astype(bf16)
        out.append(dict(
            norm1_g=w["norm1_g"][l][None],
            w_main=w_main, w_kv=wi[OFF_CK:N_IN].astype(bf16),
            a_norm_g=w["a_norm_g"][l].reshape(1, D_A),
            a_ws=w["a_ws"][l], a_bs=w["a_bs"][l],
            b_conv_w=w["b_conv_w"][l],
            alog=_pad_lanes(w["b_a_log"][l]), dtb=_pad_lanes(w["b_dt_bias"][l]),
            b_norm_g=w["b_norm_g"][l][None],
            lamv=jnp.stack([w["c_lam_q1"][l], w["c_lam_k1"][l], w["c_lam_q2"][l], w["c_lam_k2"][l]]),
            c_norm_g=w["c_norm_g"][l][None],
            w_out=w["w_out"][l].astype(bf16),
            norm2_g=w["norm2_g"][l][None],
            w_up=w["w_up"][l].astype(bf16),
            ffn_conv_w=w["ffn_conv_w"][l], ffn_conv_b=w["ffn_conv_b"][l][None],
            w_down=w["w_down"][l].astype(bf16),
        ))
    return out


def _blocks(m):
    return dict(
        in_rows=min(2 * ACC_ROWS, m), in_cols=N_MAIN // 4,
        rows=min(ACC_ROWS, m),
        ffn_cols=2 * MXU_WIDTH,
        attn=ACC_ROWS,
    )


def _run_trunk(x, lw, final_g, nseq, seq_rows, t_valid, past):
    m = x.shape[0]
    sample = past is not None
    blk = _blocks(m)
    tm = blk["rows"]
    kbuf = jnp.zeros((nseq, DEPTH, H_C, seq_rows, HEAD_DIM), f32)
    vbuf = jnp.zeros((nseq, DEPTH, H_C, seq_rows, HEAD_DIM), f32)
    deltas, convs, ffns, chunk_vs = [], [], [], []
    for l in range(DEPTH):
        w = lw[l]
        lam_init = 0.8 - 0.6 * math.exp(-0.3 * l)
        p, h = _in_proj(x, w["norm1_g"], w["w_main"], blk["in_rows"], blk["in_cols"])
        kbuf, vbuf = _kv_proj(h, w["w_kv"], kbuf, vbuf, l, seq_rows, tm)
        if sample:
            a_w = jnp.tile(w["a_ws"][:, :seq_rows, :seq_rows], (1, nseq, nseq))
            a_b = jnp.tile(w["a_bs"][:, :seq_rows].T, (nseq, 1))
            oa, va = _chunk_mlp(p, w["a_norm_g"], a_w, jnp.pad(a_b, ((0, 0), (0, LANES - H_A))),
                                m, seq_rows, m, f32)
            ob, s_new, conv_new = _delta(
                p, past["state_conv_qkv"][:, l], past["state_delta"][:, l], w["b_conv_w"], w["alog"],
                w["dtb"], w["b_norm_g"], nseq, seq_rows, t_valid, 1, f32)
            oc = _paged_attn(p, kbuf, vbuf, w["lamv"], w["c_norm_g"], past["cache_k"], past["cache_v"],
                             past["page_table"], l, PAGES_PER_STEP, lam_init)
            ffn_hist = past["state_ffn_conv"][:, l]
            nseq_blk = nseq
        else:
            a_b = jnp.pad(w["a_bs"].T, ((0, 0), (0, LANES - H_A)))
            oa, va = _chunk_mlp(p, w["a_norm_g"], w["a_ws"], a_b, CHUNK_A, CHUNK_A, tm, bf16)
            ob, s_new, conv_new = _delta(
                p, jnp.zeros((nseq, CONV_B - 1, 3 * D_B), f32),
                jnp.zeros((nseq, H_B, HEAD_DIM, HEAD_DIM), f32), w["b_conv_w"], w["alog"], w["dtb"],
                w["b_norm_g"], nseq, CHUNK_B, CHUNK_B, seq_rows // CHUNK_B, bf16)
            oc = _attn(p, kbuf, vbuf, l, w["lamv"], w["c_norm_g"], nseq, seq_rows,
                       min(blk["attn"], seq_rows), lam_init)
            ffn_hist = jnp.zeros((nseq, CONV_FFN - 1, 2 * D_FF), f32)
            nseq_blk = 1
        x, h2 = _out_proj(oa, ob, oc, x, w["w_out"], w["norm2_g"], tm)
        act, hg, hv = _ffn_up(h2, w["w_up"], w["ffn_conv_w"], w["ffn_conv_b"], ffn_hist,
                              nseq_blk, seq_rows, t_valid, blk["ffn_cols"])
        x = _ffn_down(act, w["w_down"], x, final_g[None], tm, l == DEPTH - 1)
        deltas.append(s_new)
        convs.append(conv_new)
        ffns.append(jnp.concatenate([hg, hv], axis=-1))
        if sample:
            chunk_vs.append(va.reshape(nseq, seq_rows, H_A, HEAD_DIM)[:, :t_valid])
    y = x.reshape(nseq, seq_rows, D_MODEL)[:, :t_valid]
    k_rows = jnp.transpose(kbuf, (0, 1, 3, 2, 4))[:, :, :t_valid]
    v_rows = jnp.transpose(vbuf, (0, 1, 3, 2, 4))[:, :, :t_valid]
    stack = lambda xs: jnp.stack(xs, axis=1)
    deltas = jnp.transpose(jnp.stack(deltas, axis=0), (1, 0, 2, 3, 4))
    convs = jnp.transpose(jnp.stack(convs, axis=2), (0, 2, 1, 3))
    return (y, k_rows, v_rows, deltas, convs, stack(ffns), stack(chunk_vs) if sample else None)


def kernel(x_prompt, x_sample, cache_k, cache_v, page_table, state_delta, state_conv_qkv, state_ffn_conv,
           norm1_g, w_in, a_norm_g, a_ws, a_bs, b_conv_w, b_a_log, b_dt_bias, b_norm_g, c_lam_q1, c_lam_k1,
           c_lam_q2, c_lam_k2, c_norm_g, w_out, norm2_g, w_up, ffn_conv_w, ffn_conv_b, w_down, final_g):
    weights = dict(norm1_g=norm1_g, w_in=w_in, a_norm_g=a_norm_g, a_ws=a_ws, a_bs=a_bs, b_conv_w=b_conv_w,
                   b_a_log=b_a_log, b_dt_bias=b_dt_bias, b_norm_g=b_norm_g, c_lam_q1=c_lam_q1,
                   c_lam_k1=c_lam_k1, c_lam_q2=c_lam_q2, c_lam_k2=c_lam_k2, c_norm_g=c_norm_g, w_out=w_out,
                   norm2_g=norm2_g, w_up=w_up, ffn_conv_w=ffn_conv_w, ffn_conv_b=ffn_conv_b, w_down=w_down)
    lw = _prep_weights(weights)

    batch, seq, _ = x_prompt.shape
    yp, pk, pv, pd, pc, pf, _ = _run_trunk(
        x_prompt.reshape(batch * seq, D_MODEL), lw, final_g, batch, seq, seq, None)

    dbatch, dseq, _ = x_sample.shape
    xs = jnp.pad(x_sample, ((0, 0), (0, SAMPLE_ROWS - dseq), (0, 0))).reshape(dbatch * SAMPLE_ROWS, D_MODEL)
    past = dict(
        cache_k=jnp.transpose(cache_k, (0, 1, 3, 2, 4)),
        cache_v=jnp.transpose(cache_v, (0, 1, 3, 2, 4)),
        page_table=page_table.reshape(-1).astype(jnp.int32),
        state_delta=state_delta, state_conv_qkv=state_conv_qkv, state_ffn_conv=state_ffn_conv)
    ys, sk, sv, sd, sc, sf, scv = _run_trunk(xs, lw, final_g, dbatch, SAMPLE_ROWS, dseq, past)
    return (yp, ys, pk, pv, pd, pc, pf, sk, sv, sd, sc, sf, scv)
```

```python
import functools
import math

import jax
import jax.numpy as jnp
from jax import lax
from jax.experimental import pallas as pl
from jax.experimental.pallas import tpu as pltpu

f32 = jnp.float32
bf16 = jnp.bfloat16

D_MODEL = 2048
DEPTH = 4
HEAD_DIM = 128
H_A = 4
D_A = H_A * HEAD_DIM
CHUNK_A = 128
H_B = 6
D_B = H_B * HEAD_DIM
H_C = 6
D_C = H_C * HEAD_DIM
DC_HALF = HEAD_DIM // 2
CONV_B = 4
CHUNK_B = 64
D_FF = 5632
CONV_FFN = 3
EPS = 1e-6
PAGE_SIZE = 128
SAMPLE_ROWS = 8
PAGES_PER_STEP = 8

OFF_AU = 0
OFF_AV = OFF_AU + D_A
OFF_BQKV = OFF_AV + D_A
OFF_BZ = OFF_BQKV + 3 * D_B
OFF_BA = OFF_BZ + D_B
OFF_BB = OFF_BA + H_B
OFF_CQ = OFF_BB + H_B
OFF_CK = OFF_CQ + D_C
OFF_CV = OFF_CK + D_C
N_IN = OFF_CV + D_C

P_BQKV = 0
P_BZ = P_BQKV + 3 * D_B
P_CQ = P_BZ + D_B
P_AU = P_CQ + D_C
P_AV = P_AU + D_A
P_AB = P_AV + D_A
N_MAIN = P_AB + 256
N_KV = 2 * D_C
LANES = 128

MXU_WIDTH = 256
FFN_SUB = MXU_WIDTH
ACC_ROWS = 512
FFN_ROWS = ACC_ROWS
VMEM_LIMIT = 56 * 1024 * 1024
NEG_BIG = -1e30
HI = lax.Precision.HIGHEST
NT = (((1,), (1,)), ((), ()))
TN = (((0,), (0,)), ((), ()))


def _params(*sem):
    return pltpu.CompilerParams(dimension_semantics=sem, vmem_limit_bytes=VMEM_LIMIT)


def _sigmoid(x):
    return 1.0 / (1.0 + jnp.exp(-x))


def _gelu_tanh(x):
    c = math.sqrt(2.0 / math.pi)
    return 0.5 * x * (1.0 + jnp.tanh(c * (x + 0.044715 * (x * x * x))))


def _rms(x, g):
    return x * lax.rsqrt(jnp.mean(x * x, axis=-1, keepdims=True) + EPS) * g


def _in_proj_kernel(x_ref, g_ref, w_ref, p_ref, h_ref):
    @pl.when(pl.program_id(1) == 0)
    def _():
        h_ref[...] = _rms(x_ref[...], g_ref[...]).astype(bf16)

    tm, tn = p_ref.shape
    rb = min(tm, ACC_ROWS)
    for n in range(tn // MXU_WIDTH):
        cols = slice(n * MXU_WIDTH, (n + 1) * MXU_WIDTH)
        for r in range(0, tm, rb):
            p_ref[r:r + rb, cols] = lax.dot_general(h_ref[r:r + rb, :], w_ref[cols, :], NT,
                                                    preferred_element_type=f32)


def _in_proj(x, g, w, layer, tm, tn):
    m = x.shape[0]
    n = w.shape[1]
    return pl.pallas_call(
        _in_proj_kernel,
        grid=(m // tm, n // tn),
        in_specs=[
            pl.BlockSpec((tm, D_MODEL), lambda i, j: (i, 0)),
            pl.BlockSpec((1, D_MODEL), lambda i, j: (0, 0)),
            pl.BlockSpec((None, tn, D_MODEL), lambda i, j: (layer, j, 0)),
        ],
        out_specs=[
            pl.BlockSpec((tm, tn), lambda i, j: (i, j)),
            pl.BlockSpec((tm, D_MODEL), lambda i, j: (i, 0)),
        ],
        out_shape=[jax.ShapeDtypeStruct((m, n), f32), jax.ShapeDtypeStruct((m, D_MODEL), bf16)],
        compiler_params=_params("parallel", "arbitrary"),
        name="in_proj",
    )(x, g, w)


def _kv_proj_kernel(h_ref, w_ref, kin_ref, vin_ref, k_ref, v_ref):
    del kin_ref, vin_ref
    h = h_ref[...]
    for n in range(N_KV // MXU_WIDTH):
        res = lax.dot_general(h, w_ref[n * MXU_WIDTH:(n + 1) * MXU_WIDTH, :], NT, preferred_element_type=f32)
        for half in range(MXU_WIDTH // HEAD_DIM):
            head = n * (MXU_WIDTH // HEAD_DIM) + half
            dst = k_ref if head < H_C else v_ref
            piece = res[:, half * HEAD_DIM:(half + 1) * HEAD_DIM]
            if len(dst.shape) == 3:
                dst[head % H_C] = piece
            else:
                rows = dst.shape[2]
                for b in range(dst.shape[0]):
                    dst[b, head % H_C] = piece[b * rows:(b + 1) * rows]


def _kv_proj(h, w_kv, kbuf, vbuf, layer, seq_rows, tm):
    m = h.shape[0]
    if tm <= seq_rows:
        per_seq = seq_rows // tm
        buf_spec = pl.BlockSpec((None, None, H_C, tm, HEAD_DIM),
                                lambda i: (i // per_seq, layer, 0, i % per_seq, 0))
    else:
        buf_spec = pl.BlockSpec((tm // seq_rows, None, H_C, seq_rows, HEAD_DIM), lambda i: (i, layer, 0, 0, 0))
    return pl.pallas_call(
        _kv_proj_kernel,
        grid=(m // tm,),
        in_specs=[
            pl.BlockSpec((tm, D_MODEL), lambda i: (i, 0)),
            pl.BlockSpec((None, N_KV, D_MODEL), lambda i: (layer, 0, 0), pipeline_mode=pl.Buffered(1)),
            pl.BlockSpec(memory_space=pl.ANY),
            pl.BlockSpec(memory_space=pl.ANY),
        ],
        out_specs=[buf_spec, buf_spec],
        out_shape=[jax.ShapeDtypeStruct(kbuf.shape, f32), jax.ShapeDtypeStruct(vbuf.shape, f32)],
        input_output_aliases={2: 0, 3: 1},
        compiler_params=_params("parallel"),
        name="kv_proj",
    )(h, w_kv, kbuf, vbuf)


def _chunk_mlp_kernel(u0_ref, u1_ref, v0_ref, v1_ref, g_ref, w_ref, bs_ref, oa_ref, va_ref, *, r, seq_shift):
    nsub = u0_ref.shape[0] // r
    row = lax.broadcasted_iota(jnp.int32, (r, r), 0)
    col = lax.broadcasted_iota(jnp.int32, (r, r), 1)
    same_seq = lax.shift_right_logical(row, seq_shift) == lax.shift_right_logical(col, seq_shift)
    mask = jnp.logical_and(same_seq, col <= row)
    g = g_ref[...]
    for h in range(H_A):
        wm = jnp.where(mask, w_ref[h], 0.0).astype(bf16)
        u_ref = (u0_ref, u1_ref)[h // 2]
        v_ref = (v0_ref, v1_ref)[h // 2]
        lo = (h % 2) * HEAD_DIM
        for c in range(nsub):
            rows = slice(c * r, (c + 1) * r)
            u = _gelu_tanh(u_ref[rows, lo:lo + HEAD_DIM])
            v = _rms(_gelu_tanh(v_ref[rows, lo:lo + HEAD_DIM]), g[:, h * HEAD_DIM:(h + 1) * HEAD_DIM])
            va_ref[rows, h * HEAD_DIM:(h + 1) * HEAD_DIM] = v
            mixed = jnp.dot(wm, v.astype(bf16), preferred_element_type=f32) + bs_ref[:, h:h + 1]
            oa_ref[rows, h * HEAD_DIM:(h + 1) * HEAD_DIM] = (u * mixed).astype(oa_ref.dtype)


def _chunk_mlp(p, g, w, bs_rows, r, seq_rows, rb, out_dtype):
    m = p.shape[0]
    half = 2 * HEAD_DIM
    cu, cv = P_AU // half, P_AV // half
    kern = functools.partial(_chunk_mlp_kernel, r=r, seq_shift=int(math.log2(seq_rows)))
    return pl.pallas_call(
        kern,
        grid=(m // rb,),
        in_specs=[
            pl.BlockSpec((rb, half), lambda i: (i, cu)),
            pl.BlockSpec((rb, half), lambda i: (i, cu + 1)),
            pl.BlockSpec((rb, half), lambda i: (i, cv)),
            pl.BlockSpec((rb, half), lambda i: (i, cv + 1)),
            pl.BlockSpec((1, D_A), lambda i: (0, 0)),
            pl.BlockSpec((H_A, r, r), lambda i: (0, 0, 0)),
            pl.BlockSpec((r, LANES), lambda i: (0, 0)),
        ],
        out_specs=[
            pl.BlockSpec((rb, D_A), lambda i: (i, 0)),
            pl.BlockSpec((rb, D_A), lambda i: (i, 0)),
        ],
        out_shape=[jax.ShapeDtypeStruct((m, D_A), out_dtype), jax.ShapeDtypeStruct((m, D_A), f32)],
        compiler_params=_params("parallel"),
        name="chunk_mlp",
    )(p, p, p, p, g, w, bs_rows)


def _softplus(x):
    return jnp.maximum(x, 0.0) + jnp.log(1.0 + jnp.exp(-jnp.abs(x)))


def _mm(a, b):
    return jnp.dot(a.astype(bf16), b.astype(bf16), preferred_element_type=f32)


def _delta_kernel(x_ref, z_ref, ab_ref, hist_ref, s0_ref, cw_ref, alog_ref, dtb_ref, ng_ref,
                  o_ref, s_ref, cn_ref, ext, *, rows, t_valid, nchunks):
    c = CHUNK_B
    ci = pl.program_id(1)

    @pl.when(ci == 0)
    def _():
        ext[0:8, :] = jnp.zeros((8, 3 * D_B), f32)
        ext[5:8, :] = hist_ref[...]
        s_ref[...] = s0_ref[...]

    x = x_ref[...]
    ext[8:8 + rows, :] = x
    cw = cw_ref[...]
    y = (ext[5:5 + rows, :] * cw[0:1] + ext[6:6 + rows, :] * cw[1:2]
         + ext[7:7 + rows, :] * cw[2:3] + x * cw[3:4])

    @pl.when(ci == nchunks - 1)
    def _():
        cn_ref[...] = ext[5 + t_valid:8 + t_valid, :]

    ext[0:8, :] = ext[rows:rows + 8, :]

    y = y * _sigmoid(y)
    z = z_ref[...]
    ab = ab_ref[...]
    if rows < c:
        y = jnp.concatenate([y, jnp.zeros((c - rows, y.shape[1]), f32)], axis=0)
        z = jnp.concatenate([z, jnp.zeros((c - rows, z.shape[1]), f32)], axis=0)
        ab = jnp.concatenate([ab, jnp.zeros((c - rows, ab.shape[1]), f32)], axis=0)
    masked = t_valid < c
    rmask = lax.broadcasted_iota(jnp.int32, (c, 1), 0) < t_valid

    g_all = -jnp.exp(alog_ref[...]) * _softplus(ab + dtb_ref[...])
    beta_all = _sigmoid(ab)
    if masked:
        g_all = jnp.where(rmask, g_all, 0.0)
        beta_all = jnp.where(rmask, beta_all, 0.0)

    row = lax.broadcasted_iota(jnp.int32, (c, c), 0)
    col = lax.broadcasted_iota(jnp.int32, (c, c), 1)
    incl = row >= col
    strict = row > col
    eye = (row == col).astype(f32)
    gc_all = jnp.dot(incl.astype(f32), g_all, precision=HI, preferred_element_type=f32)
    egc_all = jnp.exp(gc_all)
    blk8 = lax.shift_right_logical(row, 3) == lax.shift_right_logical(col, 3)
    lvl_masks = []
    for sh in range(3, int(math.log2(c))):
        rb = lax.shift_right_logical(row, sh)
        cb = lax.shift_right_logical(col, sh)
        lvl_masks.append(jnp.logical_and(jnp.bitwise_and(rb, 1) == 1, cb == rb - 1))
    gc_t = jnp.transpose(jnp.concatenate([gc_all, jnp.zeros((LANES - c, LANES), f32)], axis=0))
    ng = ng_ref[...]

    hs = range(H_B)
    hd = lambda h, base=0: slice(base + h * HEAD_DIM, base + (h + 1) * HEAD_DIM)

    q = [y[:, hd(h)] for h in hs]
    k = [y[:, hd(h, D_B)] for h in hs]
    v = [y[:, hd(h, 2 * D_B)] for h in hs]
    q = [t * lax.rsqrt(jnp.sum(t * t, axis=-1, keepdims=True) + EPS) * (HEAD_DIM ** -0.5) for t in q]
    k = [t * lax.rsqrt(jnp.sum(t * t, axis=-1, keepdims=True) + EPS) for t in k]
    if masked:
        q = [jnp.where(rmask, t, 0.0) for t in q]
        k = [jnp.where(rmask, t, 0.0) for t in k]
        v = [jnp.where(rmask, t, 0.0) for t in v]
    gc = [gc_all[:, h:h + 1] for h in hs]
    egc = [egc_all[:, h:h + 1] for h in hs]
    beta = [beta_all[:, H_B + h:H_B + h + 1] for h in hs]
    decay = [jnp.exp(jnp.where(incl, gc[h] - gc_t[h:h + 1, 0:c], NEG_BIG)) for h in hs]
    kb = [k[h] * beta[h] for h in hs]
    vb = [v[h] * beta[h] for h in hs]
    k16 = [t.astype(bf16) for t in k]
    aq = [lax.dot_general(jnp.concatenate([kb[h], q[h]], axis=0).astype(bf16), k16[h], NT,
                          preferred_element_type=f32) for h in hs]
    a = [aq[h][:c] * jnp.where(strict, decay[h], 0.0) for h in hs]
    qk = [aq[h][c:] * decay[h] for h in hs]
    x = [jnp.where(blk8, -t, 0.0) for t in a]
    tinv = [eye + t for t in x]
    p1 = [_mm(t, t) for t in x]
    pt = [_mm(jnp.concatenate([p1[h], tinv[h]], axis=0), p1[h]) for h in hs]
    tinv = [tinv[h] + pt[h][c:] for h in hs]
    t2 = [_mm(tinv[h], pt[h][:c]) for h in hs]
    tinv = [tinv[h] + t2[h] for h in hs]
    for msk in lvl_masks:
        lt = [_mm(jnp.where(msk, a[h], 0.0), tinv[h]) for h in hs]
        tl = [_mm(tinv[h], lt[h]) for h in hs]
        tinv = [tinv[h] - tl[h] for h in hs]
    rhs = [jnp.concatenate([vb[h], kb[h] * egc[h]], axis=1) for h in hs]
    sol = [rhs[h] + _mm(tinv[h] - eye, rhs[h]) for h in hs]
    s = [s_ref[h] for h in hs]
    wq = [_mm(jnp.concatenate([sol[h][:, HEAD_DIM:], q[h] * egc[h]], axis=0), s[h]) for h in hs]
    vn16 = [(sol[h][:, :HEAD_DIM] - wq[h][:c]).astype(bf16) for h in hs]
    o = [wq[h][c:] + jnp.dot(qk[h].astype(bf16), vn16[h], preferred_element_type=f32) for h in hs]
    g_last = [gc_all[c - 1:c, h:h + 1] for h in hs]
    kd = [(k[h] * jnp.exp(g_last[h] - gc[h])).astype(bf16) for h in hs]
    ds = [lax.dot_general(kd[h], vn16[h], TN, preferred_element_type=f32) for h in hs]
    for h in hs:
        s_ref[h] = s[h] * jnp.exp(g_last[h]) + ds[h]
    for h in hs:
        zz = z[:, hd(h)]
        oh = _rms(o[h], ng) * (zz * _sigmoid(zz))
        o_ref[:, hd(h)] = oh[:rows].astype(o_ref.dtype)


def _delta(p, hist, s0, cw, alog, dtb, ng, nseq, rows, t_valid, nchunks, out_dtype):
    m = p.shape[0]
    kern = functools.partial(_delta_kernel, rows=rows, t_valid=t_valid, nchunks=nchunks)
    return pl.pallas_call(
        kern,
        grid=(nseq, nchunks),
        in_specs=[
            pl.BlockSpec((rows, 3 * D_B), lambda b, c: (b * nchunks + c, P_BQKV // (3 * D_B))),
            pl.BlockSpec((rows, D_B), lambda b, c: (b * nchunks + c, P_BZ // D_B)),
            pl.BlockSpec((rows, LANES), lambda b, c: (b * nchunks + c, P_AB // LANES)),
            pl.BlockSpec((None, CONV_B - 1, 3 * D_B), lambda b, c: (b, 0, 0)),
            pl.BlockSpec((None, H_B, HEAD_DIM, HEAD_DIM), lambda b, c: (b, 0, 0, 0)),
            pl.BlockSpec((CONV_B, 3 * D_B), lambda b, c: (0, 0)),
            pl.BlockSpec((1, LANES), lambda b, c: (0, 0)),
            pl.BlockSpec((1, LANES), lambda b, c: (0, 0)),
            pl.BlockSpec((1, HEAD_DIM), lambda b, c: (0, 0)),
        ],
        out_specs=[
            pl.BlockSpec((rows, D_B), lambda b, c: (b * nchunks + c, 0)),
            pl.BlockSpec((None, H_B, HEAD_DIM, HEAD_DIM), lambda b, c: (b, 0, 0, 0)),
            pl.BlockSpec((None, CONV_B - 1, 3 * D_B), lambda b, c: (b, 0, 0)),
        ],
        out_shape=[
            jax.ShapeDtypeStruct((m, D_B), out_dtype),
            jax.ShapeDtypeStruct((nseq, H_B, HEAD_DIM, HEAD_DIM), f32),
            jax.ShapeDtypeStruct((nseq, CONV_B - 1, 3 * D_B), f32),
        ],
        scratch_shapes=[pltpu.VMEM((rows + 8, 3 * D_B), f32)],
        compiler_params=_params("parallel", "arbitrary"),
        name="delta",
    )(p, p, p, hist, s0, cw, alog, dtb, ng)


def _lam(lam_ref, lam_init):
    lv = lam_ref[...]
    a = jnp.sum(lv[0:1] * lv[1:2], axis=-1, keepdims=True)
    b = jnp.sum(lv[2:3] * lv[3:4], axis=-1, keepdims=True)
    return jnp.exp(a) - jnp.exp(b) + lam_init


def _attn_kernel(lam_ref, q_ref, k_ref, v_ref, g_ref, o_ref, qs, m_s, l_s, acc, *, t, lam_init):
    qi = pl.program_id(2)
    q = q_ref[...] * (DC_HALF ** -0.5)
    lane = lax.broadcasted_iota(jnp.int32, q.shape, 1)
    qs[0] = jnp.where(lane < DC_HALF, q, 0.0).astype(bf16)
    qs[1] = jnp.where(lane >= DC_HALF, q, 0.0).astype(bf16)
    m_s[...] = jnp.full(m_s.shape, NEG_BIG, f32)
    l_s[...] = jnp.zeros(l_s.shape, f32)
    acc[...] = jnp.zeros(acc.shape, f32)

    def block(j, diag):
        off = pl.multiple_of(j * t, t)
        k = k_ref[pl.ds(off, t), :].astype(bf16)
        v = v_ref[pl.ds(off, t), :].astype(bf16)
        cs = range(2)
        s = [lax.dot_general(qs[c], k, NT, preferred_element_type=f32) for c in cs]
        if diag:
            row = lax.broadcasted_iota(jnp.int32, (t, t), 0)
            col = lax.broadcasted_iota(jnp.int32, (t, t), 1)
            s = [jnp.where(col <= row, sc, NEG_BIG) for sc in s]
        m_old = [m_s[c] for c in cs]
        m_new = [jnp.maximum(m_old[c], jnp.max(s[c], axis=-1, keepdims=True)) for c in cs]
        alpha = [jnp.exp(m_old[c] - m_new[c]) for c in cs]
        p = [jnp.exp(s[c] - jnp.concatenate([m_new[c]] * (t // HEAD_DIM), axis=1)) for c in cs]
        pv = [jnp.dot(p[c].astype(bf16), v, preferred_element_type=f32) for c in cs]
        for c in cs:
            l_s[c] = alpha[c] * l_s[c] + jnp.sum(p[c], axis=-1, keepdims=True)
            acc[c] = alpha[c] * acc[c] + pv[c]
            m_s[c] = m_new[c]

    def body(j, carry):
        block(j, False)
        return carry

    lax.fori_loop(0, qi, body, 0)
    block(qi, True)

    o = acc[0] / l_s[0] - _lam(lam_ref, lam_init) * (acc[1] / l_s[1])
    o = _rms(o, g_ref[...]) * (1.0 - lam_init)
    o_ref[...] = o.astype(o_ref.dtype)


def _attn(p, kbuf, vbuf, layer, lamv, g, nseq, seq, t, lam_init):
    m = p.shape[0]
    nb = seq // t
    cq = P_CQ // HEAD_DIM
    kern = functools.partial(_attn_kernel, t=t, lam_init=lam_init)
    kv_spec = pl.BlockSpec((None, None, None, seq, HEAD_DIM), lambda b, h, i: (b, layer, h, 0, 0))
    return pl.pallas_call(
        kern,
        grid=(nseq, H_C, nb),
        in_specs=[
            pl.BlockSpec((4, DC_HALF), lambda b, h, i: (0, 0)),
            pl.BlockSpec((t, HEAD_DIM), lambda b, h, i: (b * nb + i, cq + h)),
            kv_spec,
            kv_spec,
            pl.BlockSpec((1, HEAD_DIM), lambda b, h, i: (0, 0)),
        ],
        out_specs=pl.BlockSpec((t, HEAD_DIM), lambda b, h, i: (b * nb + i, h)),
        out_shape=jax.ShapeDtypeStruct((m, D_C), bf16),
        scratch_shapes=[
            pltpu.VMEM((2, t, HEAD_DIM), bf16),
            pltpu.VMEM((2, t, HEAD_DIM), f32),
            pltpu.VMEM((2, t, HEAD_DIM), f32),
            pltpu.VMEM((2, t, HEAD_DIM), f32),
        ],
        compiler_params=_params("parallel", "parallel", "arbitrary"),
        name="attn",
    )(lamv, p, kbuf, vbuf, g)


def _paged_attn_kernel(pt_ref, lam_ref, q_ref, kn_ref, vn_ref, g_ref, *rest, npg, lam_init):
    k_refs = rest[:npg]
    v_refs = rest[npg:2 * npg]
    o_ref = rest[2 * npg]
    qs, kb, vb, m_s, l_s, acc = rest[2 * npg + 1:]
    gi = pl.program_id(1)
    hq = 2 * SAMPLE_ROWS

    @pl.when(gi == 0)
    def _():
        q = q_ref[...] * (DC_HALF ** -0.5)
        lane = lax.broadcasted_iota(jnp.int32, (SAMPLE_ROWS, HEAD_DIM), 1)
        for h in range(H_C):
            qh = q[:, h * HEAD_DIM:(h + 1) * HEAD_DIM]
            qs[h] = jnp.concatenate([jnp.where(lane < DC_HALF, qh, 0.0),
                                     jnp.where(lane >= DC_HALF, qh, 0.0)], axis=0).astype(bf16)
        m_s[...] = jnp.full(m_s.shape, NEG_BIG, f32)
        l_s[...] = jnp.zeros(l_s.shape, f32)
        acc[...] = jnp.zeros(acc.shape, f32)

    def update(s, v_of_head):
        m_old = m_s[...]
        m_new = jnp.maximum(m_old, jnp.max(s, axis=-1, keepdims=True))
        alpha = jnp.exp(m_old - m_new)
        p = jnp.exp(s - jnp.concatenate([m_new] * (s.shape[1] // LANES), axis=1))
        l_s[...] = alpha * l_s[...] + jnp.sum(p, axis=-1, keepdims=True)
        p16 = p.astype(bf16)
        pv = jnp.concatenate(
            [jnp.dot(p16[h * hq:(h + 1) * hq], v_of_head(h), preferred_element_type=f32) for h in range(H_C)],
            axis=0)
        acc[...] = alpha * acc[...] + pv
        m_s[...] = m_new

    for i in range(npg):
        for h in range(H_C):
            kb[h, i * PAGE_SIZE:(i + 1) * PAGE_SIZE, :] = k_refs[i][h].astype(bf16)
            vb[h, i * PAGE_SIZE:(i + 1) * PAGE_SIZE, :] = v_refs[i][h].astype(bf16)
    s_all = jnp.concatenate(
        [lax.dot_general(qs[h], kb[h], NT, preferred_element_type=f32) for h in range(H_C)], axis=0)
    update(s_all, lambda h: vb[h])

    @pl.when(gi == pl.num_programs(1) - 1)
    def _():
        pad = jnp.zeros((LANES - SAMPLE_ROWS, HEAD_DIM), f32)
        head = lambda x_ref, h: jnp.concatenate([x_ref[h], pad], axis=0).astype(bf16)
        s = jnp.concatenate(
            [lax.dot_general(qs[h], head(kn_ref, h), NT, preferred_element_type=f32) for h in range(H_C)], axis=0)
        row = lax.broadcasted_iota(jnp.int32, s.shape, 0)
        col = lax.broadcasted_iota(jnp.int32, s.shape, 1)
        s = jnp.where(col <= jnp.bitwise_and(row, SAMPLE_ROWS - 1), s, NEG_BIG)
        update(s, lambda h: head(vn_ref, h))
        o12 = acc[...] / l_s[...]
        lam = _lam(lam_ref, lam_init)
        for h in range(H_C):
            r1 = h * hq
            r2 = r1 + SAMPLE_ROWS
            o = o12[r1:r1 + SAMPLE_ROWS] - lam * o12[r2:r2 + SAMPLE_ROWS]
            o_ref[:, h * HEAD_DIM:(h + 1) * HEAD_DIM] = _rms(o, g_ref[...]) * (1.0 - lam_init)


def _paged_attn(p, kbuf, vbuf, lamv, g, cache_k, cache_v, page_table, layer, npg, lam_init):
    m = p.shape[0]
    nseq = m // SAMPLE_ROWS
    n_pages = page_table.shape[0] // nseq
    ngroups = n_pages // npg
    nq = 2 * H_C * SAMPLE_ROWS

    def page_spec(i):
        return pl.BlockSpec((None, None, H_C, PAGE_SIZE, HEAD_DIM),
                            lambda b, gq, pt: (pt[b * n_pages + gq * npg + i], layer, 0, 0, 0))

    new_spec = pl.BlockSpec((None, None, H_C, SAMPLE_ROWS, HEAD_DIM), lambda b, gq, pt: (b, layer, 0, 0, 0))
    kern = functools.partial(_paged_attn_kernel, npg=npg, lam_init=lam_init)
    grid_spec = pltpu.PrefetchScalarGridSpec(
        num_scalar_prefetch=1,
        grid=(nseq, ngroups),
        in_specs=[
            pl.BlockSpec((4, DC_HALF), lambda b, gq, pt: (0, 0)),
            pl.BlockSpec((SAMPLE_ROWS, D_C), lambda b, gq, pt: (b, P_CQ // D_C)), new_spec, new_spec,
            pl.BlockSpec((1, HEAD_DIM), lambda b, gq, pt: (0, 0)),
        ] + [page_spec(i) for i in range(npg)] + [page_spec(i) for i in range(npg)],
        out_specs=pl.BlockSpec((SAMPLE_ROWS, D_C), lambda b, gq, pt: (b, 0)),
        scratch_shapes=[
            pltpu.VMEM((H_C, 2 * SAMPLE_ROWS, HEAD_DIM), bf16),
            pltpu.VMEM((H_C, npg * PAGE_SIZE, HEAD_DIM), bf16),
            pltpu.VMEM((H_C, npg * PAGE_SIZE, HEAD_DIM), bf16),
            pltpu.VMEM((nq, LANES), f32),
            pltpu.VMEM((nq, LANES), f32),
            pltpu.VMEM((nq, HEAD_DIM), f32),
        ],
    )
    return pl.pallas_call(
        kern,
        grid_spec=grid_spec,
        out_shape=jax.ShapeDtypeStruct((m, D_C), f32),
        compiler_params=_params("parallel", "arbitrary"),
        name="paged_attn",
    )(page_table, lamv, p, kbuf, vbuf, g, *([cache_k] * npg), *([cache_v] * npg))


def _out_proj_kernel(a_ref, b_ref, c_ref, x_ref, w_ref, g_ref, xo_ref, h_ref):
    mix = jnp.concatenate([a_ref[...].astype(bf16), b_ref[...].astype(bf16), c_ref[...].astype(bf16)], axis=1)
    for n in range(D_MODEL // MXU_WIDTH):
        cols = slice(n * MXU_WIDTH, (n + 1) * MXU_WIDTH)
        xo_ref[:, cols] = x_ref[:, cols] + jnp.dot(mix, w_ref[:, cols], preferred_element_type=f32)
    h_ref[...] = _rms(xo_ref[...], g_ref[...]).astype(bf16)


def _out_proj(a, b, c, x, wo, layer, g, tm):
    m = x.shape[0]
    row = lambda w: pl.BlockSpec((tm, w), lambda i: (i, 0))
    return pl.pallas_call(
        _out_proj_kernel,
        grid=(m // tm,),
        in_specs=[
            row(D_A), row(D_B), row(D_C), row(D_MODEL),
            pl.BlockSpec((None, D_MODEL, D_MODEL), lambda i: (layer, 0, 0), pipeline_mode=pl.Buffered(1)),
            pl.BlockSpec((1, D_MODEL), lambda i: (0, 0)),
        ],
        out_specs=[row(D_MODEL), row(D_MODEL)],
        out_shape=[jax.ShapeDtypeStruct((m, D_MODEL), f32), jax.ShapeDtypeStruct((m, D_MODEL), bf16)],
        compiler_params=_params("parallel"),
        name="out_proj",
    )(a, b, c, x, wo, g)


def _ffn_up_kernel(h_ref, wg_ref, wv_ref, cwg_ref, cwv_ref, cbg_ref, cbv_ref, hg_ref, hv_ref,
                   act_ref, ng_ref, nv_ref, *, nseq, seq_rows, t_valid):
    h = h_ref[...]
    tf = wg_ref.shape[1]
    ts = min(tf, FFN_SUB)
    rin = lax.broadcasted_iota(jnp.int32, (nseq, seq_rows, ts), 1)

    def conv(w_ref, cw_ref, cb_ref, hist_ref, new_ref, cols):
        w = w_ref[:, cols]
        rb = min(h.shape[0], FFN_ROWS)
        up = jnp.concatenate([jnp.dot(h[r:r + rb], w, preferred_element_type=f32)
                              for r in range(0, h.shape[0], rb)], axis=0)
        up3 = up.reshape(nseq, seq_rows, ts)
        r1 = pltpu.roll(up, 1, axis=0).reshape(nseq, seq_rows, ts)
        r2 = pltpu.roll(up, 2, axis=0).reshape(nseq, seq_rows, ts)
        hist = hist_ref[:, :, cols]
        h0 = hist[:, 0:1, :]
        h1 = hist[:, 1:2, :]
        prev1 = jnp.where(rin == 0, h1, r1)
        prev2 = jnp.where(rin == 0, h0, jnp.where(rin == 1, h1, r2))
        cw = cw_ref[:, cols]
        y = prev2 * cw[0:1] + prev1 * cw[1:2] + up3 * cw[2:3] + cb_ref[:, cols]
        new_ref[:, :, cols] = up3[:, t_valid - 2:t_valid, :]
        return y

    for sub in range(tf // ts):
        cols = slice(sub * ts, (sub + 1) * ts)
        yg = conv(wg_ref, cwg_ref, cbg_ref, hg_ref, ng_ref, cols)
        yv = conv(wv_ref, cwv_ref, cbv_ref, hv_ref, nv_ref, cols)
        act = yg * _sigmoid(yg) * yv
        act_ref[:, cols] = act.reshape(nseq * seq_rows, ts).astype(bf16)


def _ffn_up(h2, w_up, layer, cw, cb, hist, nseq_blk, seq_rows, t_valid, tf):
    m = h2.shape[0]
    tm = nseq_blk * seq_rows
    nseq = m // seq_rows
    nj = D_FF // tf
    kern = functools.partial(_ffn_up_kernel, nseq=nseq_blk, seq_rows=seq_rows, t_valid=t_valid)
    gate = lambda i, j: (0, j)
    val = lambda i, j: (0, nj + j)
    return pl.pallas_call(
        kern,
        grid=(m // tm, nj),
        in_specs=[
            pl.BlockSpec((tm, D_MODEL), lambda i, j: (i, 0)),
            pl.BlockSpec((None, D_MODEL, tf), lambda i, j: (layer, 0, j)),
            pl.BlockSpec((None, D_MODEL, tf), lambda i, j: (layer, 0, nj + j)),
            pl.BlockSpec((CONV_FFN, tf), gate),
            pl.BlockSpec((CONV_FFN, tf), val),
            pl.BlockSpec((1, tf), gate),
            pl.BlockSpec((1, tf), val),
            pl.BlockSpec((nseq_blk, CONV_FFN - 1, tf), lambda i, j: (i, 0, j)),
            pl.BlockSpec((nseq_blk, CONV_FFN - 1, tf), lambda i, j: (i, 0, nj + j)),
        ],
        out_specs=[
            pl.BlockSpec((tm, tf), lambda i, j: (i, j)),
            pl.BlockSpec((nseq_blk, CONV_FFN - 1, tf), lambda i, j: (i, 0, j)),
            pl.BlockSpec((nseq_blk, CONV_FFN - 1, tf), lambda i, j: (i, 0, j)),
        ],
        out_shape=[
            jax.ShapeDtypeStruct((m, D_FF), bf16),
            jax.ShapeDtypeStruct((nseq, CONV_FFN - 1, D_FF), f32),
            jax.ShapeDtypeStruct((nseq, CONV_FFN - 1, D_FF), f32),
        ],
        compiler_params=_params("parallel", "arbitrary"),
        name="ffn_up",
    )(h2, w_up, w_up, cw, cw, cb, cb, hist, hist)


def _ffn_down_kernel(a_ref, w_ref, x_ref, g_ref, o_ref, *, final_norm):
    a = a_ref[...]
    for n in range(D_MODEL // MXU_WIDTH):
        cols = slice(n * MXU_WIDTH, (n + 1) * MXU_WIDTH)
        o_ref[:, cols] = x_ref[:, cols] + jnp.dot(a, w_ref[:, cols], preferred_element_type=f32)
    if final_norm:
        o_ref[...] = _rms(o_ref[...], g_ref[...])


def _ffn_down(act, w_down, layer, x, g, tm, final_norm):
    m = x.shape[0]
    kern = functools.partial(_ffn_down_kernel, final_norm=final_norm)
    return pl.pallas_call(
        kern,
        grid=(m // tm,),
        in_specs=[
            pl.BlockSpec((tm, D_FF), lambda i: (i, 0)),
            pl.BlockSpec((None, D_FF, D_MODEL), lambda i: (layer, 0, 0), pipeline_mode=pl.Buffered(1)),
            pl.BlockSpec((tm, D_MODEL), lambda i: (i, 0)),
            pl.BlockSpec((1, D_MODEL), lambda i: (0, 0)),
        ],
        out_specs=pl.BlockSpec((tm, D_MODEL), lambda i: (i, 0)),
        out_shape=jax.ShapeDtypeStruct((m, D_MODEL), f32),
        compiler_params=_params("parallel"),
        name="ffn_down",
    )(act, w_down, x, g)


def _pad_lanes(v, offset=0):
    return jnp.zeros((1, LANES), f32).at[0, offset:offset + v.shape[0]].set(v)


def _prep_weights(w):
    wt = jnp.swapaxes(w["w_in"], 1, 2)
    stacked = dict(
        w_main=jnp.concatenate(
            [wt[:, OFF_BQKV:OFF_BA], wt[:, OFF_CQ:OFF_CK], wt[:, OFF_AU:OFF_BQKV], wt[:, OFF_BA:OFF_CQ],
             jnp.zeros((DEPTH, N_MAIN - P_AB - 2 * H_B, D_MODEL), f32)], axis=1).astype(bf16),
        w_kv=wt[:, OFF_CK:N_IN].astype(bf16),
        w_out=w["w_out"].astype(bf16), w_up=w["w_up"].astype(bf16), w_down=w["w_down"].astype(bf16))
    out = []
    for l in range(DEPTH):
        out.append(dict(
            norm1_g=w["norm1_g"][l][None],
            a_norm_g=w["a_norm_g"][l].reshape(1, D_A),
            a_ws=w["a_ws"][l], a_bs=w["a_bs"][l],
            b_conv_w=w["b_conv_w"][l],
            alog=_pad_lanes(w["b_a_log"][l]), dtb=_pad_lanes(w["b_dt_bias"][l]),
            b_norm_g=w["b_norm_g"][l][None],
            lamv=jnp.stack([w["c_lam_q1"][l], w["c_lam_k1"][l], w["c_lam_q2"][l], w["c_lam_k2"][l]]),
            c_norm_g=w["c_norm_g"][l][None],
            norm2_g=w["norm2_g"][l][None],
            ffn_conv_w=w["ffn_conv_w"][l], ffn_conv_b=w["ffn_conv_b"][l][None],
        ))
    return out, stacked


def _blocks(m):
    return dict(
        in_rows=min(2 * ACC_ROWS, m), in_cols=N_MAIN // 4,
        rows=min(ACC_ROWS, m),
        ffn_cols=2 * MXU_WIDTH,
        attn=ACC_ROWS,
    )


def _run_trunk(x, lw, sw, final_g, nseq, seq_rows, t_valid, past):
    m = x.shape[0]
    sample = past is not None
    blk = _blocks(m)
    tm = blk["rows"]
    kbuf = jnp.zeros((nseq, DEPTH, H_C, seq_rows, HEAD_DIM), f32)
    vbuf = jnp.zeros((nseq, DEPTH, H_C, seq_rows, HEAD_DIM), f32)
    deltas, convs, ffns, chunk_vs = [], [], [], []
    for l in range(DEPTH):
        w = lw[l]
        lam_init = 0.8 - 0.6 * math.exp(-0.3 * l)
        p, h = _in_proj(x, w["norm1_g"], sw["w_main"], l, blk["in_rows"], blk["in_cols"])
        kbuf, vbuf = _kv_proj(h, sw["w_kv"], kbuf, vbuf, l, seq_rows, tm)
        if sample:
            a_w = jnp.tile(w["a_ws"][:, :seq_rows, :seq_rows], (1, nseq, nseq))
            a_b = jnp.tile(w["a_bs"][:, :seq_rows].T, (nseq, 1))
            oa, va = _chunk_mlp(p, w["a_norm_g"], a_w, jnp.pad(a_b, ((0, 0), (0, LANES - H_A))),
                                m, seq_rows, m, f32)
            ob, s_new, conv_new = _delta(
                p, past["state_conv_qkv"][:, l], past["state_delta"][:, l], w["b_conv_w"], w["alog"],
                w["dtb"], w["b_norm_g"], nseq, seq_rows, t_valid, 1, f32)
            oc = _paged_attn(p, kbuf, vbuf, w["lamv"], w["c_norm_g"], past["cache_k"], past["cache_v"],
                             past["page_table"], l, PAGES_PER_STEP, lam_init)
            ffn_hist = past["state_ffn_conv"][:, l]
            nseq_blk = nseq
        else:
            a_b = jnp.pad(w["a_bs"].T, ((0, 0), (0, LANES - H_A)))
            oa, va = _chunk_mlp(p, w["a_norm_g"], w["a_ws"], a_b, CHUNK_A, CHUNK_A, tm, bf16)
            ob, s_new, conv_new = _delta(
                p, jnp.zeros((nseq, CONV_B - 1, 3 * D_B), f32),
                jnp.zeros((nseq, H_B, HEAD_DIM, HEAD_DIM), f32), w["b_conv_w"], w["alog"], w["dtb"],
                w["b_norm_g"], nseq, CHUNK_B, CHUNK_B, seq_rows // CHUNK_B, bf16)
            oc = _attn(p, kbuf, vbuf, l, w["lamv"], w["c_norm_g"], nseq, seq_rows,
                       min(blk["attn"], seq_rows), lam_init)
            ffn_hist = jnp.zeros((nseq, CONV_FFN - 1, 2 * D_FF), f32)
            nseq_blk = 1
        x, h2 = _out_proj(oa, ob, oc, x, sw["w_out"], l, w["norm2_g"], tm)
        act, hg, hv = _ffn_up(h2, sw["w_up"], l, w["ffn_conv_w"], w["ffn_conv_b"], ffn_hist,
                              nseq_blk, seq_rows, t_valid, blk["ffn_cols"])
        x = _ffn_down(act, sw["w_down"], l, x, final_g[None], tm, l == DEPTH - 1)
        deltas.append(s_new)
        convs.append(conv_new)
        ffns.append(jnp.concatenate([hg, hv], axis=-1))
        if sample:
            chunk_vs.append(va.reshape(nseq, seq_rows, H_A, HEAD_DIM)[:, :t_valid])
    y = x.reshape(nseq, seq_rows, D_MODEL)[:, :t_valid]
    k_rows = jnp.transpose(kbuf, (0, 1, 3, 2, 4))[:, :, :t_valid]
    v_rows = jnp.transpose(vbuf, (0, 1, 3, 2, 4))[:, :, :t_valid]
    stack = lambda xs: jnp.stack(xs, axis=1)
    deltas = jnp.transpose(jnp.stack(deltas, axis=0), (1, 0, 2, 3, 4))
    convs = jnp.transpose(jnp.stack(convs, axis=2), (0, 2, 1, 3))
    return (y, k_rows, v_rows, deltas, convs, stack(ffns), stack(chunk_vs) if sample else None)


def kernel(x_prompt, x_sample, cache_k, cache_v, page_table, state_delta, state_conv_qkv, state_ffn_conv,
           norm1_g, w_in, a_norm_g, a_ws, a_bs, b_conv_w, b_a_log, b_dt_bias, b_norm_g, c_lam_q1, c_lam_k1,
           c_lam_q2, c_lam_k2, c_norm_g, w_out, norm2_g, w_up, ffn_conv_w, ffn_conv_b, w_down, final_g):
    weights = dict(norm1_g=norm1_g, w_in=w_in, a_norm_g=a_norm_g, a_ws=a_ws, a_bs=a_bs, b_conv_w=b_conv_w,
                   b_a_log=b_a_log, b_dt_bias=b_dt_bias, b_norm_g=b_norm_g, c_lam_q1=c_lam_q1,
                   c_lam_k1=c_lam_k1, c_lam_q2=c_lam_q2, c_lam_k2=c_lam_k2, c_norm_g=c_norm_g, w_out=w_out,
                   norm2_g=norm2_g, w_up=w_up, ffn_conv_w=ffn_conv_w, ffn_conv_b=ffn_conv_b, w_down=w_down)
    lw, sw = _prep_weights(weights)

    batch, seq, _ = x_prompt.shape
    yp, pk, pv, pd, pc, pf, _ = _run_trunk(
        x_prompt.reshape(batch * seq, D_MODEL), lw, sw, final_g, batch, seq, seq, None)

    dbatch, dseq, _ = x_sample.shape
    xs = jnp.pad(x_sample, ((0, 0), (0, SAMPLE_ROWS - dseq), (0, 0))).reshape(dbatch * SAMPLE_ROWS, D_MODEL)
    past = dict(
        cache_k=jnp.transpose(cache_k, (0, 1, 3, 2, 4)),
        cache_v=jnp.transpose(cache_v, (0, 1, 3, 2, 4)),
        page_table=page_table.reshape(-1).astype(jnp.int32),
        state_delta=state_delta, state_conv_qkv=state_conv_qkv, state_ffn_conv=state_ffn_conv)
    ys, sk, sv, sd, sc, sf, scv = _run_trunk(xs, lw, sw, final_g, dbatch, SAMPLE_ROWS, dseq, past)
    return (yp, ys, pk, pv, pd, pc, pf, sk, sv, sd, sc, sf, scv)
```

```python
import functools
import math

import jax
import jax.numpy as jnp
from jax import lax
from jax.experimental import pallas as pl
from jax.experimental.pallas import tpu as pltpu

f32 = jnp.float32
bf16 = jnp.bfloat16

D_MODEL = 2048
DEPTH = 4
HEAD_DIM = 128
H_A = 4
D_A = H_A * HEAD_DIM
CHUNK_A = 128
H_B = 6
D_B = H_B * HEAD_DIM
H_C = 6
D_C = H_C * HEAD_DIM
DC_HALF = HEAD_DIM // 2
CONV_B = 4
CHUNK_B = 64
D_FF = 5632
CONV_FFN = 3
EPS = 1e-6
PAGE_SIZE = 128
SAMPLE_ROWS = 8
DELTA_SEQS = 2
PAGES_PER_STEP = 8

OFF_AU = 0
OFF_AV = OFF_AU + D_A
OFF_BQKV = OFF_AV + D_A
OFF_BZ = OFF_BQKV + 3 * D_B
OFF_BA = OFF_BZ + D_B
OFF_BB = OFF_BA + H_B
OFF_CQ = OFF_BB + H_B
OFF_CK = OFF_CQ + D_C
OFF_CV = OFF_CK + D_C
N_IN = OFF_CV + D_C

IN_COLS = 1024
P_BQKV = 0
P_BZ = P_BQKV + 3 * D_B
P_AU = P_BZ + D_B
P_AV = P_AU + D_A
P_CQ = P_AV + D_A
P_AB = P_CQ + D_C
N_MAIN = P_CQ + IN_COLS
N_KV = 2 * D_C
assert OFF_BQKV == IN_COLS and OFF_BA - OFF_BQKV == 3 * IN_COLS and P_AU == 3 * IN_COLS
LANES = 128

MXU_WIDTH = 256
FFN_SUB = MXU_WIDTH
ACC_ROWS = 512
FFN_ROWS = ACC_ROWS
VMEM_LIMIT = 56 * 1024 * 1024
NEG_BIG = -1e30
HI = lax.Precision.HIGHEST
NT = (((1,), (1,)), ((), ()))
TN = (((0,), (0,)), ((), ()))


def _params(*sem):
    return pltpu.CompilerParams(dimension_semantics=sem, vmem_limit_bytes=VMEM_LIMIT)


def _sigmoid(x):
    return 1.0 / (1.0 + jnp.exp(-x))


def _gelu_tanh(x):
    c = math.sqrt(2.0 / math.pi)
    return 0.5 * x * (1.0 + jnp.tanh(c * (x + 0.044715 * (x * x * x))))


def _rms(x, g):
    return x * lax.rsqrt(jnp.mean(x * x, axis=-1, keepdims=True) + EPS) * g


def _in_proj_kernel(x_ref, g_ref, w_ref, wq_ref, p_ref, h_ref):
    j = pl.program_id(1)

    @pl.when(j == 0)
    def _():
        h_ref[...] = _rms(x_ref[...], g_ref[...]).astype(bf16)

    tm, tn = p_ref.shape
    rb = min(tm, ACC_ROWS)

    def project(src_ref):
        for n in range(tn // MXU_WIDTH):
            cols = slice(n * MXU_WIDTH, (n + 1) * MXU_WIDTH)
            for r in range(0, tm, rb):
                p_ref[r:r + rb, cols] = lax.dot_general(h_ref[r:r + rb, :], src_ref[cols, :], NT,
                                                        preferred_element_type=f32)

    last = pl.num_programs(1) - 1

    @pl.when(j < last)
    def _():
        project(w_ref)

    @pl.when(j == last)
    def _():
        project(wq_ref)


def _in_proj(x, g, wt, wq, layer, tm):
    m = x.shape[0]
    nsteps = N_MAIN // IN_COLS
    src_block = lambda j: jnp.where(j < 3, j + 1, 0)
    return pl.pallas_call(
        _in_proj_kernel,
        grid=(m // tm, nsteps),
        in_specs=[
            pl.BlockSpec((tm, D_MODEL), lambda i, j: (i, 0)),
            pl.BlockSpec((1, D_MODEL), lambda i, j: (0, 0)),
            pl.BlockSpec((None, IN_COLS, D_MODEL), lambda i, j: (layer, src_block(j), 0)),
            pl.BlockSpec((None, IN_COLS, D_MODEL), lambda i, j: (layer, 0, 0)),
        ],
        out_specs=[
            pl.BlockSpec((tm, IN_COLS), lambda i, j: (i, j)),
            pl.BlockSpec((tm, D_MODEL), lambda i, j: (i, 0)),
        ],
        out_shape=[jax.ShapeDtypeStruct((m, N_MAIN), f32), jax.ShapeDtypeStruct((m, D_MODEL), bf16)],
        compiler_params=_params("parallel", "arbitrary"),
        name="in_proj",
    )(x, g, wt, wq)


def _kv_proj_kernel(h_ref, w_ref, kin_ref, vin_ref, k_ref, v_ref):
    del kin_ref, vin_ref
    h = h_ref[...]
    for n in range(N_KV // MXU_WIDTH):
        res = lax.dot_general(h, w_ref[n * MXU_WIDTH:(n + 1) * MXU_WIDTH, :], NT, preferred_element_type=f32)
        for half in range(MXU_WIDTH // HEAD_DIM):
            head = n * (MXU_WIDTH // HEAD_DIM) + half
            dst = k_ref if head < H_C else v_ref
            piece = res[:, half * HEAD_DIM:(half + 1) * HEAD_DIM]
            if len(dst.shape) == 3:
                dst[head % H_C] = piece
            else:
                rows = dst.shape[2]
                for b in range(dst.shape[0]):
                    dst[b, head % H_C] = piece[b * rows:(b + 1) * rows]


def _kv_proj(h, w_kv, kbuf, vbuf, layer, seq_rows, tm):
    m = h.shape[0]
    if tm <= seq_rows:
        per_seq = seq_rows // tm
        buf_spec = pl.BlockSpec((None, None, H_C, tm, HEAD_DIM),
                                lambda i: (i // per_seq, layer, 0, i % per_seq, 0))
    else:
        buf_spec = pl.BlockSpec((tm // seq_rows, None, H_C, seq_rows, HEAD_DIM), lambda i: (i, layer, 0, 0, 0))
    return pl.pallas_call(
        _kv_proj_kernel,
        grid=(m // tm,),
        in_specs=[
            pl.BlockSpec((tm, D_MODEL), lambda i: (i, 0)),
            pl.BlockSpec((None, N_KV, D_MODEL), lambda i: (layer, 0, 0), pipeline_mode=pl.Buffered(1)),
            pl.BlockSpec(memory_space=pl.ANY),
            pl.BlockSpec(memory_space=pl.ANY),
        ],
        out_specs=[buf_spec, buf_spec],
        out_shape=[jax.ShapeDtypeStruct(kbuf.shape, f32), jax.ShapeDtypeStruct(vbuf.shape, f32)],
        input_output_aliases={2: 0, 3: 1},
        compiler_params=_params("parallel"),
        name="kv_proj",
    )(h, w_kv, kbuf, vbuf)


def _chunk_mlp_kernel(u0_ref, u1_ref, v0_ref, v1_ref, g_ref, w_ref, bs_ref, oa_ref, va_ref, *, r, seq_shift):
    nsub = u0_ref.shape[0] // r
    row = lax.broadcasted_iota(jnp.int32, (r, r), 0)
    col = lax.broadcasted_iota(jnp.int32, (r, r), 1)
    same_seq = lax.shift_right_logical(row, seq_shift) == lax.shift_right_logical(col, seq_shift)
    mask = jnp.logical_and(same_seq, col <= row)
    g = g_ref[...]
    for h in range(H_A):
        wm = jnp.where(mask, w_ref[h], 0.0).astype(bf16)
        u_ref = (u0_ref, u1_ref)[h // 2]
        v_ref = (v0_ref, v1_ref)[h // 2]
        lo = (h % 2) * HEAD_DIM
        for c in range(nsub):
            rows = slice(c * r, (c + 1) * r)
            u = _gelu_tanh(u_ref[rows, lo:lo + HEAD_DIM])
            v = _rms(_gelu_tanh(v_ref[rows, lo:lo + HEAD_DIM]), g[:, h * HEAD_DIM:(h + 1) * HEAD_DIM])
            va_ref[rows, h * HEAD_DIM:(h + 1) * HEAD_DIM] = v
            mixed = jnp.dot(wm, v.astype(bf16), preferred_element_type=f32) + bs_ref[:, h:h + 1]
            oa_ref[rows, h * HEAD_DIM:(h + 1) * HEAD_DIM] = (u * mixed).astype(oa_ref.dtype)


def _chunk_mlp(p, g, w, bs_rows, r, seq_rows, rb, out_dtype):
    m = p.shape[0]
    half = 2 * HEAD_DIM
    cu, cv = P_AU // half, P_AV // half
    kern = functools.partial(_chunk_mlp_kernel, r=r, seq_shift=int(math.log2(seq_rows)))
    return pl.pallas_call(
        kern,
        grid=(m // rb,),
        in_specs=[
            pl.BlockSpec((rb, half), lambda i: (i, cu)),
            pl.BlockSpec((rb, half), lambda i: (i, cu + 1)),
            pl.BlockSpec((rb, half), lambda i: (i, cv)),
            pl.BlockSpec((rb, half), lambda i: (i, cv + 1)),
            pl.BlockSpec((1, D_A), lambda i: (0, 0)),
            pl.BlockSpec((H_A, r, r), lambda i: (0, 0, 0)),
            pl.BlockSpec((r, LANES), lambda i: (0, 0)),
        ],
        out_specs=[
            pl.BlockSpec((rb, D_A), lambda i: (i, 0)),
            pl.BlockSpec((rb, D_A), lambda i: (i, 0)),
        ],
        out_shape=[jax.ShapeDtypeStruct((m, D_A), out_dtype), jax.ShapeDtypeStruct((m, D_A), f32)],
        compiler_params=_params("parallel"),
        name="chunk_mlp",
    )(p, p, p, p, g, w, bs_rows)


def _softplus(x):
    return jnp.maximum(x, 0.0) + jnp.log(1.0 + jnp.exp(-jnp.abs(x)))


def _mm(a, b):
    return jnp.dot(a.astype(bf16), b.astype(bf16), preferred_element_type=f32)


def _delta_kernel(x_ref, z_ref, ab_ref, hist_ref, s0_ref, cw_ref, alog_ref, dtb_ref, ng_ref,
                  o_ref, s_ref, cn_ref, ext, *, rows, t_valid, nchunks):
    c = CHUNK_B
    nsb = x_ref.shape[0]
    ci = pl.program_id(1)
    masked = t_valid < c
    rmask = lax.broadcasted_iota(jnp.int32, (c, 1), 0) < t_valid
    row = lax.broadcasted_iota(jnp.int32, (c, c), 0)
    col = lax.broadcasted_iota(jnp.int32, (c, c), 1)
    incl = row >= col
    strict = row > col
    eye = (row == col).astype(f32)
    blk8 = lax.shift_right_logical(row, 3) == lax.shift_right_logical(col, 3)
    lvl_masks = []
    for sh in range(3, int(math.log2(c))):
        rb = lax.shift_right_logical(row, sh)
        cb = lax.shift_right_logical(col, sh)
        lvl_masks.append(jnp.logical_and(jnp.bitwise_and(rb, 1) == 1, cb == rb - 1))
    ng = ng_ref[...]
    cw = cw_ref[...]

    @pl.when(ci == 0)
    def _():
        ext[:, 0:8, :] = jnp.zeros((nsb, 8, 3 * D_B), f32)
        ext[:, 5:8, :] = hist_ref[...]
        s_ref[...] = s0_ref[...]

    ys, zs, g_alls, beta_alls, gc_alls, egc_alls, gc_ts = [], [], [], [], [], [], []
    for sb in range(nsb):
        x = x_ref[sb]
        ext[sb, 8:8 + rows, :] = x
        y = (ext[sb, 5:5 + rows, :] * cw[0:1] + ext[sb, 6:6 + rows, :] * cw[1:2]
             + ext[sb, 7:7 + rows, :] * cw[2:3] + x * cw[3:4])

        @pl.when(ci == nchunks - 1)
        def _():
            cn_ref[sb] = ext[sb, 5 + t_valid:8 + t_valid, :]

        ext[sb, 0:8, :] = ext[sb, rows:rows + 8, :]

        y = y * _sigmoid(y)
        z = z_ref[sb]
        ab = ab_ref[sb]
        if rows < c:
            y = jnp.concatenate([y, jnp.zeros((c - rows, y.shape[1]), f32)], axis=0)
            z = jnp.concatenate([z, jnp.zeros((c - rows, z.shape[1]), f32)], axis=0)
            ab = jnp.concatenate([ab, jnp.zeros((c - rows, ab.shape[1]), f32)], axis=0)
        g_all = -jnp.exp(alog_ref[...]) * _softplus(ab + dtb_ref[...])
        beta_all = _sigmoid(ab)
        if masked:
            g_all = jnp.where(rmask, g_all, 0.0)
            beta_all = jnp.where(rmask, beta_all, 0.0)
        gc_all = jnp.dot(incl.astype(f32), g_all, precision=HI, preferred_element_type=f32)
        ys.append(y)
        zs.append(z)
        g_alls.append(g_all)
        beta_alls.append(beta_all)
        gc_alls.append(gc_all)
        egc_alls.append(jnp.exp(gc_all))
        gc_ts.append(jnp.transpose(jnp.concatenate([gc_all, jnp.zeros((LANES - c, LANES), f32)], axis=0)))

    hs = range(nsb * H_B)
    sq = lambda i: i // H_B
    hh = lambda i: i % H_B
    hd = lambda h, base=0: slice(base + h * HEAD_DIM, base + (h + 1) * HEAD_DIM)

    q = [ys[sq(i)][:, hd(hh(i))] for i in hs]
    k = [ys[sq(i)][:, hd(hh(i), D_B)] for i in hs]
    v = [ys[sq(i)][:, hd(hh(i), 2 * D_B)] for i in hs]
    q = [t * lax.rsqrt(jnp.sum(t * t, axis=-1, keepdims=True) + EPS) * (HEAD_DIM ** -0.5) for t in q]
    k = [t * lax.rsqrt(jnp.sum(t * t, axis=-1, keepdims=True) + EPS) for t in k]
    if masked:
        q = [jnp.where(rmask, t, 0.0) for t in q]
        k = [jnp.where(rmask, t, 0.0) for t in k]
        v = [jnp.where(rmask, t, 0.0) for t in v]
    gc = [gc_alls[sq(i)][:, hh(i):hh(i) + 1] for i in hs]
    egc = [egc_alls[sq(i)][:, hh(i):hh(i) + 1] for i in hs]
    beta = [beta_alls[sq(i)][:, H_B + hh(i):H_B + hh(i) + 1] for i in hs]
    decay = [jnp.exp(jnp.where(incl, gc[i] - gc_ts[sq(i)][hh(i):hh(i) + 1, 0:c], NEG_BIG)) for i in hs]
    kb = [k[h] * beta[h] for h in hs]
    vb = [v[h] * beta[h] for h in hs]
    k16 = [t.astype(bf16) for t in k]
    aq = [lax.dot_general(jnp.concatenate([kb[h], q[h]], axis=0).astype(bf16), k16[h], NT,
                          preferred_element_type=f32) for h in hs]
    a = [aq[h][:c] * jnp.where(strict, decay[h], 0.0) for h in hs]
    qk = [aq[h][c:] * decay[h] for h in hs]
    x = [jnp.where(blk8, -t, 0.0) for t in a]
    tinv = [eye + t for t in x]
    p1 = [_mm(t, t) for t in x]
    pt = [_mm(jnp.concatenate([p1[h], tinv[h]], axis=0), p1[h]) for h in hs]
    tinv = [tinv[h] + pt[h][c:] for h in hs]
    t2 = [_mm(tinv[h], pt[h][:c]) for h in hs]
    tinv = [tinv[h] + t2[h] for h in hs]
    for msk in lvl_masks:
        lt = [_mm(jnp.where(msk, a[h], 0.0), tinv[h]) for h in hs]
        tl = [_mm(tinv[h], lt[h]) for h in hs]
        tinv = [tinv[h] - tl[h] for h in hs]
    rhs = [jnp.concatenate([vb[h], kb[h] * egc[h]], axis=1) for h in hs]
    sol = [rhs[h] + _mm(tinv[h] - eye, rhs[h]) for h in hs]
    s = [s_ref[sq(i), hh(i)] for i in hs]
    wq = [_mm(jnp.concatenate([sol[h][:, HEAD_DIM:], q[h] * egc[h]], axis=0), s[h]) for h in hs]
    vn16 = [(sol[h][:, :HEAD_DIM] - wq[h][:c]).astype(bf16) for h in hs]
    o = [wq[h][c:] + jnp.dot(qk[h].astype(bf16), vn16[h], preferred_element_type=f32) for h in hs]
    g_last = [gc_alls[sq(i)][c - 1:c, hh(i):hh(i) + 1] for i in hs]
    kd = [(k[h] * jnp.exp(g_last[h] - gc[h])).astype(bf16) for h in hs]
    ds = [lax.dot_general(kd[h], vn16[h], TN, preferred_element_type=f32) for h in hs]
    for i in hs:
        s_ref[sq(i), hh(i)] = s[i] * jnp.exp(g_last[i]) + ds[i]
    for i in hs:
        zz = zs[sq(i)][:, hd(hh(i))]
        oh = _rms(o[i], ng) * (zz * _sigmoid(zz))
        o_ref[sq(i), :, hd(hh(i))] = oh[:rows].astype(o_ref.dtype)


def _delta(p, hist, s0, cw, alog, dtb, ng, nseq, rows, t_valid, nchunks, out_dtype):
    m = p.shape[0]
    nsb = DELTA_SEQS
    p3 = p.reshape(nseq, m // nseq, p.shape[1])
    kern = functools.partial(_delta_kernel, rows=rows, t_valid=t_valid, nchunks=nchunks)
    ob, s_new, conv_new = pl.pallas_call(
        kern,
        grid=(nseq // nsb, nchunks),
        in_specs=[
            pl.BlockSpec((nsb, rows, 3 * D_B), lambda b, c: (b, c, P_BQKV // (3 * D_B))),
            pl.BlockSpec((nsb, rows, D_B), lambda b, c: (b, c, P_BZ // D_B)),
            pl.BlockSpec((nsb, rows, LANES), lambda b, c: (b, c, P_AB // LANES)),
            pl.BlockSpec((nsb, CONV_B - 1, 3 * D_B), lambda b, c: (b, 0, 0)),
            pl.BlockSpec((nsb, H_B, HEAD_DIM, HEAD_DIM), lambda b, c: (b, 0, 0, 0)),
            pl.BlockSpec((CONV_B, 3 * D_B), lambda b, c: (0, 0)),
            pl.BlockSpec((1, LANES), lambda b, c: (0, 0)),
            pl.BlockSpec((1, LANES), lambda b, c: (0, 0)),
            pl.BlockSpec((1, HEAD_DIM), lambda b, c: (0, 0)),
        ],
        out_specs=[
            pl.BlockSpec((nsb, rows, D_B), lambda b, c: (b, c, 0)),
            pl.BlockSpec((nsb, H_B, HEAD_DIM, HEAD_DIM), lambda b, c: (b, 0, 0, 0)),
            pl.BlockSpec((nsb, CONV_B - 1, 3 * D_B), lambda b, c: (b, 0, 0)),
        ],
        out_shape=[
            jax.ShapeDtypeStruct((nseq, m // nseq, D_B), out_dtype),
            jax.ShapeDtypeStruct((nseq, H_B, HEAD_DIM, HEAD_DIM), f32),
            jax.ShapeDtypeStruct((nseq, CONV_B - 1, 3 * D_B), f32),
        ],
        scratch_shapes=[pltpu.VMEM((nsb, rows + 8, 3 * D_B), f32)],
        compiler_params=_params("parallel", "arbitrary"),
        name="delta",
    )(p3, p3, p3, hist, s0, cw, alog, dtb, ng)
    return ob.reshape(m, D_B), s_new, conv_new


def _lam(lam_ref, lam_init):
    lv = lam_ref[...]
    a = jnp.sum(lv[0:1] * lv[1:2], axis=-1, keepdims=True)
    b = jnp.sum(lv[2:3] * lv[3:4], axis=-1, keepdims=True)
    return jnp.exp(a) - jnp.exp(b) + lam_init


def _attn_kernel(lam_ref, q_ref, k_ref, v_ref, g_ref, o_ref, qs, m_s, l_s, acc, *, t, lam_init):
    qi = pl.program_id(2)
    q = q_ref[...] * (DC_HALF ** -0.5)
    lane = lax.broadcasted_iota(jnp.int32, q.shape, 1)
    qs[0] = jnp.where(lane < DC_HALF, q, 0.0).astype(bf16)
    qs[1] = jnp.where(lane >= DC_HALF, q, 0.0).astype(bf16)
    m_s[...] = jnp.full(m_s.shape, NEG_BIG, f32)
    l_s[...] = jnp.zeros(l_s.shape, f32)
    acc[...] = jnp.zeros(acc.shape, f32)

    def block(j, diag):
        off = pl.multiple_of(j * t, t)
        k = k_ref[pl.ds(off, t), :].astype(bf16)
        v = v_ref[pl.ds(off, t), :].astype(bf16)
        cs = range(2)
        s = [lax.dot_general(qs[c], k, NT, preferred_element_type=f32) for c in cs]
        if diag:
            row = lax.broadcasted_iota(jnp.int32, (t, t), 0)
            col = lax.broadcasted_iota(jnp.int32, (t, t), 1)
            s = [jnp.where(col <= row, sc, NEG_BIG) for sc in s]
        m_old = [m_s[c] for c in cs]
        m_new = [jnp.maximum(m_old[c], jnp.max(s[c], axis=-1, keepdims=True)) for c in cs]
        alpha = [jnp.exp(m_old[c] - m_new[c]) for c in cs]
        p = [jnp.exp(s[c] - jnp.concatenate([m_new[c]] * (t // HEAD_DIM), axis=1)) for c in cs]
        pv = [jnp.dot(p[c].astype(bf16), v, preferred_element_type=f32) for c in cs]
        for c in cs:
            l_s[c] = alpha[c] * l_s[c] + jnp.sum(p[c], axis=-1, keepdims=True)
            acc[c] = alpha[c] * acc[c] + pv[c]
            m_s[c] = m_new[c]

    def body(j, carry):
        block(j, False)
        return carry

    lax.fori_loop(0, qi, body, 0)
    block(qi, True)

    o = acc[0] / l_s[0] - _lam(lam_ref, lam_init) * (acc[1] / l_s[1])
    o = _rms(o, g_ref[...]) * (1.0 - lam_init)
    o_ref[...] = o.astype(o_ref.dtype)


def _attn(p, kbuf, vbuf, layer, lamv, g, nseq, seq, t, lam_init):
    m = p.shape[0]
    nb = seq // t
    cq = P_CQ // HEAD_DIM
    kern = functools.partial(_attn_kernel, t=t, lam_init=lam_init)
    kv_spec = pl.BlockSpec((None, None, None, seq, HEAD_DIM), lambda b, h, i: (b, layer, h, 0, 0))
    return pl.pallas_call(
        kern,
        grid=(nseq, H_C, nb),
        in_specs=[
            pl.BlockSpec((4, DC_HALF), lambda b, h, i: (0, 0)),
            pl.BlockSpec((t, HEAD_DIM), lambda b, h, i: (b * nb + i, cq + h)),
            kv_spec,
            kv_spec,
            pl.BlockSpec((1, HEAD_DIM), lambda b, h, i: (0, 0)),
        ],
        out_specs=pl.BlockSpec((t, HEAD_DIM), lambda b, h, i: (b * nb + i, h)),
        out_shape=jax.ShapeDtypeStruct((m, D_C), bf16),
        scratch_shapes=[
            pltpu.VMEM((2, t, HEAD_DIM), bf16),
            pltpu.VMEM((2, t, HEAD_DIM), f32),
            pltpu.VMEM((2, t, HEAD_DIM), f32),
            pltpu.VMEM((2, t, HEAD_DIM), f32),
        ],
        compiler_params=_params("parallel", "parallel", "arbitrary"),
        name="attn",
    )(lamv, p, kbuf, vbuf, g)


def _paged_attn_kernel(pt_ref, lam_ref, q0_ref, q1_ref, q2_ref, kn_ref, vn_ref, g_ref, *rest, npg, lam_init):
    k_refs = rest[:npg]
    v_refs = rest[npg:2 * npg]
    o_ref = rest[2 * npg]
    qs, kb, vb, m_s, l_s, acc = rest[2 * npg + 1:]
    gi = pl.program_id(1)
    hq = 2 * SAMPLE_ROWS

    @pl.when(gi == 0)
    def _():
        q = jnp.concatenate([q0_ref[...], q1_ref[...], q2_ref[...]], axis=1) * (DC_HALF ** -0.5)
        lane = lax.broadcasted_iota(jnp.int32, (SAMPLE_ROWS, HEAD_DIM), 1)
        for h in range(H_C):
            qh = q[:, h * HEAD_DIM:(h + 1) * HEAD_DIM]
            qs[h] = jnp.concatenate([jnp.where(lane < DC_HALF, qh, 0.0),
                                     jnp.where(lane >= DC_HALF, qh, 0.0)], axis=0).astype(bf16)
        m_s[...] = jnp.full(m_s.shape, NEG_BIG, f32)
        l_s[...] = jnp.zeros(l_s.shape, f32)
        acc[...] = jnp.zeros(acc.shape, f32)

    def update(s, v_of_head):
        m_old = m_s[...]
        m_new = jnp.maximum(m_old, jnp.max(s, axis=-1, keepdims=True))
        alpha = jnp.exp(m_old - m_new)
        p = jnp.exp(s - jnp.concatenate([m_new] * (s.shape[1] // LANES), axis=1))
        l_s[...] = alpha * l_s[...] + jnp.sum(p, axis=-1, keepdims=True)
        p16 = p.astype(bf16)
        pv = jnp.concatenate(
            [jnp.dot(p16[h * hq:(h + 1) * hq], v_of_head(h), preferred_element_type=f32) for h in range(H_C)],
            axis=0)
        acc[...] = alpha * acc[...] + pv
        m_s[...] = m_new

    for i in range(npg):
        for h in range(H_C):
            kb[h, i * PAGE_SIZE:(i + 1) * PAGE_SIZE, :] = k_refs[i][h].astype(bf16)
            vb[h, i * PAGE_SIZE:(i + 1) * PAGE_SIZE, :] = v_refs[i][h].astype(bf16)
    s_all = jnp.concatenate(
        [lax.dot_general(qs[h], kb[h], NT, preferred_element_type=f32) for h in range(H_C)], axis=0)
    update(s_all, lambda h: vb[h])

    @pl.when(gi == pl.num_programs(1) - 1)
    def _():
        pad = jnp.zeros((LANES - SAMPLE_ROWS, HEAD_DIM), f32)
        head = lambda x_ref, h: jnp.concatenate([x_ref[h], pad], axis=0).astype(bf16)
        s = jnp.concatenate(
            [lax.dot_general(qs[h], head(kn_ref, h), NT, preferred_element_type=f32) for h in range(H_C)], axis=0)
        row = lax.broadcasted_iota(jnp.int32, s.shape, 0)
        col = lax.broadcasted_iota(jnp.int32, s.shape, 1)
        s = jnp.where(col <= jnp.bitwise_and(row, SAMPLE_ROWS - 1), s, NEG_BIG)
        update(s, lambda h: head(vn_ref, h))
        o12 = acc[...] / l_s[...]
        lam = _lam(lam_ref, lam_init)
        for h in range(H_C):
            r1 = h * hq
            r2 = r1 + SAMPLE_ROWS
            o = o12[r1:r1 + SAMPLE_ROWS] - lam * o12[r2:r2 + SAMPLE_ROWS]
            o_ref[:, h * HEAD_DIM:(h + 1) * HEAD_DIM] = _rms(o, g_ref[...]) * (1.0 - lam_init)


def _paged_attn(p, kbuf, vbuf, lamv, g, cache_k, cache_v, page_table, layer, npg, lam_init):
    m = p.shape[0]
    nseq = m // SAMPLE_ROWS
    n_pages = page_table.shape[0] // nseq
    ngroups = n_pages // npg
    nq = 2 * H_C * SAMPLE_ROWS

    def page_spec(i):
        return pl.BlockSpec((None, None, H_C, PAGE_SIZE, HEAD_DIM),
                            lambda b, gq, pt: (pt[b * n_pages + gq * npg + i], layer, 0, 0, 0))

    new_spec = pl.BlockSpec((None, None, H_C, SAMPLE_ROWS, HEAD_DIM), lambda b, gq, pt: (b, layer, 0, 0, 0))
    q_spec = lambda c: pl.BlockSpec((SAMPLE_ROWS, 2 * HEAD_DIM), lambda b, gq, pt: (b, P_CQ // (2 * HEAD_DIM) + c))
    kern = functools.partial(_paged_attn_kernel, npg=npg, lam_init=lam_init)
    grid_spec = pltpu.PrefetchScalarGridSpec(
        num_scalar_prefetch=1,
        grid=(nseq, ngroups),
        in_specs=[
            pl.BlockSpec((4, DC_HALF), lambda b, gq, pt: (0, 0)),
            q_spec(0), q_spec(1), q_spec(2), new_spec, new_spec,
            pl.BlockSpec((1, HEAD_DIM), lambda b, gq, pt: (0, 0)),
        ] + [page_spec(i) for i in range(npg)] + [page_spec(i) for i in range(npg)],
        out_specs=pl.BlockSpec((SAMPLE_ROWS, D_C), lambda b, gq, pt: (b, 0)),
        scratch_shapes=[
            pltpu.VMEM((H_C, 2 * SAMPLE_ROWS, HEAD_DIM), bf16),
            pltpu.VMEM((H_C, npg * PAGE_SIZE, HEAD_DIM), bf16),
            pltpu.VMEM((H_C, npg * PAGE_SIZE, HEAD_DIM), bf16),
            pltpu.VMEM((nq, LANES), f32),
            pltpu.VMEM((nq, LANES), f32),
            pltpu.VMEM((nq, HEAD_DIM), f32),
        ],
    )
    return pl.pallas_call(
        kern,
        grid_spec=grid_spec,
        out_shape=jax.ShapeDtypeStruct((m, D_C), f32),
        compiler_params=_params("parallel", "arbitrary"),
        name="paged_attn",
    )(page_table, lamv, p, p, p, kbuf, vbuf, g, *([cache_k] * npg), *([cache_v] * npg))


def _out_proj_kernel(a_ref, b_ref, c_ref, x_ref, w_ref, g_ref, xo_ref, h_ref):
    mix = jnp.concatenate([a_ref[...].astype(bf16), b_ref[...].astype(bf16), c_ref[...].astype(bf16)], axis=1)
    for n in range(D_MODEL // MXU_WIDTH):
        cols = slice(n * MXU_WIDTH, (n + 1) * MXU_WIDTH)
        xo_ref[:, cols] = x_ref[:, cols] + jnp.dot(mix, w_ref[:, cols], preferred_element_type=f32)
    h_ref[...] = _rms(xo_ref[...], g_ref[...]).astype(bf16)


def _out_proj(a, b, c, x, wo, layer, g, tm):
    m = x.shape[0]
    row = lambda w: pl.BlockSpec((tm, w), lambda i: (i, 0))
    return pl.pallas_call(
        _out_proj_kernel,
        grid=(m // tm,),
        in_specs=[
            row(D_A), row(D_B), row(D_C), row(D_MODEL),
            pl.BlockSpec((None, D_MODEL, D_MODEL), lambda i: (layer, 0, 0), pipeline_mode=pl.Buffered(1)),
            pl.BlockSpec((1, D_MODEL), lambda i: (0, 0)),
        ],
        out_specs=[row(D_MODEL), row(D_MODEL)],
        out_shape=[jax.ShapeDtypeStruct((m, D_MODEL), f32), jax.ShapeDtypeStruct((m, D_MODEL), bf16)],
        compiler_params=_params("parallel"),
        name="out_proj",
    )(a, b, c, x, wo, g)


def _ffn_up_kernel(h_ref, wg_ref, wv_ref, cwg_ref, cwv_ref, cbg_ref, cbv_ref, hg_ref, hv_ref,
                   act_ref, ng_ref, nv_ref, *, nseq, seq_rows, t_valid):
    h = h_ref[...]
    tf = wg_ref.shape[1]
    ts = min(tf, FFN_SUB)
    rin = lax.broadcasted_iota(jnp.int32, (nseq, seq_rows, ts), 1)

    def conv(w_ref, cw_ref, cb_ref, hist_ref, new_ref, cols):
        w = w_ref[:, cols]
        rb = min(h.shape[0], FFN_ROWS)
        up = jnp.concatenate([jnp.dot(h[r:r + rb], w, preferred_element_type=f32)
                              for r in range(0, h.shape[0], rb)], axis=0)
        up3 = up.reshape(nseq, seq_rows, ts)
        r1 = pltpu.roll(up, 1, axis=0).reshape(nseq, seq_rows, ts)
        r2 = pltpu.roll(up, 2, axis=0).reshape(nseq, seq_rows, ts)
        hist = hist_ref[:, :, cols]
        h0 = hist[:, 0:1, :]
        h1 = hist[:, 1:2, :]
        prev1 = jnp.where(rin == 0, h1, r1)
        prev2 = jnp.where(rin == 0, h0, jnp.where(rin == 1, h1, r2))
        cw = cw_ref[:, cols]
        y = prev2 * cw[0:1] + prev1 * cw[1:2] + up3 * cw[2:3] + cb_ref[:, cols]
        new_ref[:, :, cols] = up3[:, t_valid - 2:t_valid, :]
        return y

    for sub in range(tf // ts):
        cols = slice(sub * ts, (sub + 1) * ts)
        yg = conv(wg_ref, cwg_ref, cbg_ref, hg_ref, ng_ref, cols)
        yv = conv(wv_ref, cwv_ref, cbv_ref, hv_ref, nv_ref, cols)
        act = yg * _sigmoid(yg) * yv
        act_ref[:, cols] = act.reshape(nseq * seq_rows, ts).astype(bf16)


def _ffn_up(h2, w_up, layer, cw, cb, hist, nseq_blk, seq_rows, t_valid, tf):
    m = h2.shape[0]
    tm = nseq_blk * seq_rows
    nseq = m // seq_rows
    nj = D_FF // tf
    kern = functools.partial(_ffn_up_kernel, nseq=nseq_blk, seq_rows=seq_rows, t_valid=t_valid)
    gate = lambda i, j: (0, j)
    val = lambda i, j: (0, nj + j)
    return pl.pallas_call(
        kern,
        grid=(m // tm, nj),
        in_specs=[
            pl.BlockSpec((tm, D_MODEL), lambda i, j: (i, 0)),
            pl.BlockSpec((None, D_MODEL, tf), lambda i, j: (layer, 0, j)),
            pl.BlockSpec((None, D_MODEL, tf), lambda i, j: (layer, 0, nj + j)),
            pl.BlockSpec((CONV_FFN, tf), gate),
            pl.BlockSpec((CONV_FFN, tf), val),
            pl.BlockSpec((1, tf), gate),
            pl.BlockSpec((1, tf), val),
            pl.BlockSpec((nseq_blk, CONV_FFN - 1, tf), lambda i, j: (i, 0, j)),
            pl.BlockSpec((nseq_blk, CONV_FFN - 1, tf), lambda i, j: (i, 0, nj + j)),
        ],
        out_specs=[
            pl.BlockSpec((tm, tf), lambda i, j: (i, j)),
            pl.BlockSpec((nseq_blk, CONV_FFN - 1, tf), lambda i, j: (i, 0, j)),
            pl.BlockSpec((nseq_blk, CONV_FFN - 1, tf), lambda i, j: (i, 0, j)),
        ],
        out_shape=[
            jax.ShapeDtypeStruct((m, D_FF), bf16),
            jax.ShapeDtypeStruct((nseq, CONV_FFN - 1, D_FF), f32),
            jax.ShapeDtypeStruct((nseq, CONV_FFN - 1, D_FF), f32),
        ],
        compiler_params=_params("parallel", "arbitrary"),
        name="ffn_up",
    )(h2, w_up, w_up, cw, cw, cb, cb, hist, hist)


def _ffn_down_kernel(a_ref, w_ref, x_ref, g_ref, o_ref, *, final_norm):
    a = a_ref[...]
    for n in range(D_MODEL // MXU_WIDTH):
        cols = slice(n * MXU_WIDTH, (n + 1) * MXU_WIDTH)
        o_ref[:, cols] = x_ref[:, cols] + jnp.dot(a, w_ref[:, cols], preferred_element_type=f32)
    if final_norm:
        o_ref[...] = _rms(o_ref[...], g_ref[...])


def _ffn_down(act, w_down, layer, x, g, tm, final_norm):
    m = x.shape[0]
    kern = functools.partial(_ffn_down_kernel, final_norm=final_norm)
    return pl.pallas_call(
        kern,
        grid=(m // tm,),
        in_specs=[
            pl.BlockSpec((tm, D_FF), lambda i: (i, 0)),
            pl.BlockSpec((None, D_FF, D_MODEL), lambda i: (layer, 0, 0), pipeline_mode=pl.Buffered(1)),
            pl.BlockSpec((tm, D_MODEL), lambda i: (i, 0)),
            pl.BlockSpec((1, D_MODEL), lambda i: (0, 0)),
        ],
        out_specs=pl.BlockSpec((tm, D_MODEL), lambda i: (i, 0)),
        out_shape=jax.ShapeDtypeStruct((m, D_MODEL), f32),
        compiler_params=_params("parallel"),
        name="ffn_down",
    )(act, w_down, x, g)


def _pad_lanes(v, offset=0):
    return jnp.zeros((1, LANES), f32).at[0, offset:offset + v.shape[0]].set(v)


def _prep_weights(w):
    wt = jnp.swapaxes(w["w_in"], 1, 2).astype(bf16)
    stacked = dict(
        w_in_t=wt,
        w_q=jnp.concatenate([wt[:, OFF_CQ:OFF_CK], wt[:, OFF_BA:OFF_CQ],
                             jnp.zeros((DEPTH, N_MAIN - P_AB - 2 * H_B, D_MODEL), bf16)], axis=1),
        w_kv=wt[:, OFF_CK:N_IN],
        w_out=w["w_out"].astype(bf16), w_up=w["w_up"].astype(bf16), w_down=w["w_down"].astype(bf16))
    out = []
    for l in range(DEPTH):
        out.append(dict(
            norm1_g=w["norm1_g"][l][None],
            a_norm_g=w["a_norm_g"][l].reshape(1, D_A),
            a_ws=w["a_ws"][l], a_bs=w["a_bs"][l],
            b_conv_w=w["b_conv_w"][l],
            alog=_pad_lanes(w["b_a_log"][l]), dtb=_pad_lanes(w["b_dt_bias"][l]),
            b_norm_g=w["b_norm_g"][l][None],
            lamv=jnp.stack([w["c_lam_q1"][l], w["c_lam_k1"][l], w["c_lam_q2"][l], w["c_lam_k2"][l]]),
            c_norm_g=w["c_norm_g"][l][None],
            norm2_g=w["norm2_g"][l][None],
            ffn_conv_w=w["ffn_conv_w"][l], ffn_conv_b=w["ffn_conv_b"][l][None],
        ))
    return out, stacked


def _blocks(m):
    return dict(
        in_rows=min(2 * ACC_ROWS, m),
        rows=min(ACC_ROWS, m),
        ffn_cols=2 * MXU_WIDTH,
        attn=ACC_ROWS,
    )


def _run_trunk(x, lw, sw, final_g, nseq, seq_rows, t_valid, past):
    m = x.shape[0]
    sample = past is not None
    blk = _blocks(m)
    tm = blk["rows"]
    kbuf = jnp.zeros((nseq, DEPTH, H_C, seq_rows, HEAD_DIM), f32)
    vbuf = jnp.zeros((nseq, DEPTH, H_C, seq_rows, HEAD_DIM), f32)
    deltas, convs, ffns, chunk_vs = [], [], [], []
    for l in range(DEPTH):
        w = lw[l]
        lam_init = 0.8 - 0.6 * math.exp(-0.3 * l)
        p, h = _in_proj(x, w["norm1_g"], sw["w_in_t"], sw["w_q"], l, blk["in_rows"])
        kbuf, vbuf = _kv_proj(h, sw["w_kv"], kbuf, vbuf, l, seq_rows, tm)
        if sample:
            a_w = jnp.tile(w["a_ws"][:, :seq_rows, :seq_rows], (1, nseq, nseq))
            a_b = jnp.tile(w["a_bs"][:, :seq_rows].T, (nseq, 1))
            oa, va = _chunk_mlp(p, w["a_norm_g"], a_w, jnp.pad(a_b, ((0, 0), (0, LANES - H_A))),
                                m, seq_rows, m, f32)
            ob, s_new, conv_new = _delta(
                p, past["state_conv_qkv"][:, l], past["state_delta"][:, l], w["b_conv_w"], w["alog"],
                w["dtb"], w["b_norm_g"], nseq, seq_rows, t_valid, 1, f32)
            oc = _paged_attn(p, kbuf, vbuf, w["lamv"], w["c_norm_g"], past["cache_k"], past["cache_v"],
                             past["page_table"], l, PAGES_PER_STEP, lam_init)
            ffn_hist = past["state_ffn_conv"][:, l]
            nseq_blk = nseq
        else:
            a_b = jnp.pad(w["a_bs"].T, ((0, 0), (0, LANES - H_A)))
            oa, va = _chunk_mlp(p, w["a_norm_g"], w["a_ws"], a_b, CHUNK_A, CHUNK_A, tm, bf16)
            ob, s_new, conv_new = _delta(
                p, jnp.zeros((nseq, CONV_B - 1, 3 * D_B), f32),
                jnp.zeros((nseq, H_B, HEAD_DIM, HEAD_DIM), f32), w["b_conv_w"], w["alog"], w["dtb"],
                w["b_norm_g"], nseq, CHUNK_B, CHUNK_B, seq_rows // CHUNK_B, bf16)
            oc = _attn(p, kbuf, vbuf, l, w["lamv"], w["c_norm_g"], nseq, seq_rows,
                       min(blk["attn"], seq_rows), lam_init)
            ffn_hist = jnp.zeros((nseq, CONV_FFN - 1, 2 * D_FF), f32)
            nseq_blk = 1
        x, h2 = _out_proj(oa, ob, oc, x, sw["w_out"], l, w["norm2_g"], tm)
        act, hg, hv = _ffn_up(h2, sw["w_up"], l, w["ffn_conv_w"], w["ffn_conv_b"], ffn_hist,
                              nseq_blk, seq_rows, t_valid, blk["ffn_cols"])
        x = _ffn_down(act, sw["w_down"], l, x, final_g[None], tm, l == DEPTH - 1)
        deltas.append(s_new)
        convs.append(conv_new)
        ffns.append(jnp.concatenate([hg, hv], axis=-1))
        if sample:
            chunk_vs.append(va.reshape(nseq, seq_rows, H_A, HEAD_DIM)[:, :t_valid])
    y = x.reshape(nseq, seq_rows, D_MODEL)[:, :t_valid]
    k_rows = jnp.transpose(kbuf, (0, 1, 3, 2, 4))[:, :, :t_valid]
    v_rows = jnp.transpose(vbuf, (0, 1, 3, 2, 4))[:, :, :t_valid]
    stack = lambda xs: jnp.stack(xs, axis=1)
    deltas = jnp.transpose(jnp.stack(deltas, axis=0), (1, 0, 2, 3, 4))
    convs = jnp.transpose(jnp.stack(convs, axis=2), (0, 2, 1, 3))
    return (y, k_rows, v_rows, deltas, convs, stack(ffns), stack(chunk_vs) if sample else None)


def kernel(x_prompt, x_sample, cache_k, cache_v, page_table, state_delta, state_conv_qkv, state_ffn_conv,
           norm1_g, w_in, a_norm_g, a_ws, a_bs, b_conv_w, b_a_log, b_dt_bias, b_norm_g, c_lam_q1, c_lam_k1,
           c_lam_q2, c_lam_k2, c_norm_g, w_out, norm2_g, w_up, ffn_conv_w, ffn_conv_b, w_down, final_g):
    weights = dict(norm1_g=norm1_g, w_in=w_in, a_norm_g=a_norm_g, a_ws=a_ws, a_bs=a_bs, b_conv_w=b_conv_w,
                   b_a_log=b_a_log, b_dt_bias=b_dt_bias, b_norm_g=b_norm_g, c_lam_q1=c_lam_q1,
                   c_lam_k1=c_lam_k1, c_lam_q2=c_lam_q2, c_lam_k2=c_lam_k2, c_norm_g=c_norm_g, w_out=w_out,
                   norm2_g=norm2_g, w_up=w_up, ffn_conv_w=ffn_conv_w, ffn_conv_b=ffn_conv_b, w_down=w_down)
    lw, sw = _prep_weights(weights)

    batch, seq, _ = x_prompt.shape
    yp, pk, pv, pd, pc, pf, _ = _run_trunk(
        x_prompt.reshape(batch * seq, D_MODEL), lw, sw, final_g, batch, seq, seq, None)

    dbatch, dseq, _ = x_sample.shape
    xs = jnp.pad(x_sample, ((0, 0), (0, SAMPLE_ROWS - dseq), (0, 0))).reshape(dbatch * SAMPLE_ROWS, D_MODEL)
    past = dict(
        cache_k=jnp.transpose(cache_k, (0, 1, 3, 2, 4)),
        cache_v=jnp.transpose(cache_v, (0, 1, 3, 2, 4)),
        page_table=page_table.reshape(-1).astype(jnp.int32),
        state_delta=state_delta, state_conv_qkv=state_conv_qkv, state_ffn_conv=state_ffn_conv)
    ys, sk, sv, sd, sc, sf, scv = _run_trunk(xs, lw, sw, final_g, dbatch, SAMPLE_ROWS, dseq, past)
    return (yp, ys, pk, pv, pd, pc, pf, sk, sv, sd, sc, sf, scv)
```

```python
import functools
import math

import jax
import jax.numpy as jnp
from jax import lax
from jax.experimental import pallas as pl
from jax.experimental.pallas import tpu as pltpu

f32 = jnp.float32
bf16 = jnp.bfloat16

D_MODEL = 2048
DEPTH = 4
HEAD_DIM = 128
H_A = 4
D_A = H_A * HEAD_DIM
CHUNK_A = 128
H_B = 6
D_B = H_B * HEAD_DIM
H_C = 6
D_C = H_C * HEAD_DIM
DC_HALF = HEAD_DIM // 2
CONV_B = 4
CHUNK_B = 64
D_FF = 5632
CONV_FFN = 3
EPS = 1e-6
PAGE_SIZE = 128
SAMPLE_ROWS = 8
DELTA_SEQS = 4
PAGES_PER_STEP = 16

OFF_AU = 0
OFF_AV = OFF_AU + D_A
OFF_BQKV = OFF_AV + D_A
OFF_BZ = OFF_BQKV + 3 * D_B
OFF_BA = OFF_BZ + D_B
OFF_BB = OFF_BA + H_B
OFF_CQ = OFF_BB + H_B
OFF_CK = OFF_CQ + D_C
OFF_CV = OFF_CK + D_C
N_IN = OFF_CV + D_C

IN_COLS = 1024
P_BQKV = 0
P_BZ = P_BQKV + 3 * D_B
P_AU = P_BZ + D_B
P_AV = P_AU + D_A
P_CQ = P_AV + D_A
P_AB = P_CQ + D_C
N_MAIN = P_CQ + IN_COLS
N_KV = 2 * D_C
assert OFF_BQKV == IN_COLS and OFF_BA - OFF_BQKV == 3 * IN_COLS and P_AU == 3 * IN_COLS
LANES = 128

MXU_WIDTH = 256
FFN_SUB = MXU_WIDTH
ACC_ROWS = 512
FFN_ROWS = ACC_ROWS
VMEM_LIMIT = 56 * 1024 * 1024
NEG_BIG = -1e30
HI = lax.Precision.HIGHEST
NT = (((1,), (1,)), ((), ()))
TN = (((0,), (0,)), ((), ()))


def _params(*sem):
    return pltpu.CompilerParams(dimension_semantics=sem, vmem_limit_bytes=VMEM_LIMIT)


def _sigmoid(x):
    return 1.0 / (1.0 + jnp.exp(-x))


def _gelu_tanh(x):
    c = math.sqrt(2.0 / math.pi)
    return 0.5 * x * (1.0 + jnp.tanh(c * (x + 0.044715 * (x * x * x))))


def _rms(x, g):
    return x * lax.rsqrt(jnp.mean(x * x, axis=-1, keepdims=True) + EPS) * g


def _in_proj_kernel(x_ref, g_ref, w_ref, wq_ref, p_ref, h_ref):
    j = pl.program_id(1)

    @pl.when(j == 0)
    def _():
        h_ref[...] = _rms(x_ref[...], g_ref[...]).astype(bf16)

    tm, tn = p_ref.shape
    rb = min(tm, ACC_ROWS)

    def project(src_ref):
        for n in range(tn // MXU_WIDTH):
            cols = slice(n * MXU_WIDTH, (n + 1) * MXU_WIDTH)
            for r in range(0, tm, rb):
                p_ref[r:r + rb, cols] = lax.dot_general(h_ref[r:r + rb, :], src_ref[cols, :], NT,
                                                        preferred_element_type=f32)

    last = pl.num_programs(1) - 1

    @pl.when(j < last)
    def _():
        project(w_ref)

    @pl.when(j == last)
    def _():
        project(wq_ref)


def _in_proj(x, g, wt, wq, layer, tm):
    m = x.shape[0]
    nsteps = N_MAIN // IN_COLS
    src_block = lambda j: jnp.where(j < 3, j + 1, 0)
    return pl.pallas_call(
        _in_proj_kernel,
        grid=(m // tm, nsteps),
        in_specs=[
            pl.BlockSpec((tm, D_MODEL), lambda i, j: (i, 0)),
            pl.BlockSpec((1, D_MODEL), lambda i, j: (0, 0)),
            pl.BlockSpec((None, IN_COLS, D_MODEL), lambda i, j: (layer, src_block(j), 0)),
            pl.BlockSpec((None, IN_COLS, D_MODEL), lambda i, j: (layer, 0, 0)),
        ],
        out_specs=[
            pl.BlockSpec((tm, IN_COLS), lambda i, j: (i, j)),
            pl.BlockSpec((tm, D_MODEL), lambda i, j: (i, 0)),
        ],
        out_shape=[jax.ShapeDtypeStruct((m, N_MAIN), f32), jax.ShapeDtypeStruct((m, D_MODEL), bf16)],
        compiler_params=_params("parallel", "arbitrary"),
        name="in_proj",
    )(x, g, wt, wq)


def _kv_proj_kernel(h_ref, w_ref, kin_ref, vin_ref, k_ref, v_ref):
    del kin_ref, vin_ref
    h = h_ref[...]
    for n in range(N_KV // MXU_WIDTH):
        res = lax.dot_general(h, w_ref[n * MXU_WIDTH:(n + 1) * MXU_WIDTH, :], NT, preferred_element_type=f32)
        for half in range(MXU_WIDTH // HEAD_DIM):
            head = n * (MXU_WIDTH // HEAD_DIM) + half
            dst = k_ref if head < H_C else v_ref
            piece = res[:, half * HEAD_DIM:(half + 1) * HEAD_DIM]
            if len(dst.shape) == 3:
                dst[head % H_C] = piece
            else:
                rows = dst.shape[2]
                for b in range(dst.shape[0]):
                    dst[b, head % H_C] = piece[b * rows:(b + 1) * rows]


def _kv_proj(h, w_kv, kbuf, vbuf, layer, seq_rows, tm):
    m = h.shape[0]
    if tm <= seq_rows:
        per_seq = seq_rows // tm
        buf_spec = pl.BlockSpec((None, None, H_C, tm, HEAD_DIM),
                                lambda i: (i // per_seq, layer, 0, i % per_seq, 0))
    else:
        buf_spec = pl.BlockSpec((tm // seq_rows, None, H_C, seq_rows, HEAD_DIM), lambda i: (i, layer, 0, 0, 0))
    return pl.pallas_call(
        _kv_proj_kernel,
        grid=(m // tm,),
        in_specs=[
            pl.BlockSpec((tm, D_MODEL), lambda i: (i, 0)),
            pl.BlockSpec((None, N_KV, D_MODEL), lambda i: (layer, 0, 0), pipeline_mode=pl.Buffered(1)),
            pl.BlockSpec(memory_space=pl.ANY),
            pl.BlockSpec(memory_space=pl.ANY),
        ],
        out_specs=[buf_spec, buf_spec],
        out_shape=[jax.ShapeDtypeStruct(kbuf.shape, f32), jax.ShapeDtypeStruct(vbuf.shape, f32)],
        input_output_aliases={2: 0, 3: 1},
        compiler_params=_params("parallel"),
        name="kv_proj",
    )(h, w_kv, kbuf, vbuf)


def _chunk_mlp_kernel(u0_ref, u1_ref, v0_ref, v1_ref, g_ref, w_ref, bs_ref, oa_ref, va_ref, *, r, seq_shift):
    nsub = u0_ref.shape[0] // r
    row = lax.broadcasted_iota(jnp.int32, (r, r), 0)
    col = lax.broadcasted_iota(jnp.int32, (r, r), 1)
    same_seq = lax.shift_right_logical(row, seq_shift) == lax.shift_right_logical(col, seq_shift)
    mask = jnp.logical_and(same_seq, col <= row)
    g = g_ref[...]
    for h in range(H_A):
        wm = jnp.where(mask, w_ref[h], 0.0).astype(bf16)
        u_ref = (u0_ref, u1_ref)[h // 2]
        v_ref = (v0_ref, v1_ref)[h // 2]
        lo = (h % 2) * HEAD_DIM
        for c in range(nsub):
            rows = slice(c * r, (c + 1) * r)
            u = _gelu_tanh(u_ref[rows, lo:lo + HEAD_DIM])
            v = _rms(_gelu_tanh(v_ref[rows, lo:lo + HEAD_DIM]), g[:, h * HEAD_DIM:(h + 1) * HEAD_DIM])
            va_ref[rows, h * HEAD_DIM:(h + 1) * HEAD_DIM] = v
            mixed = jnp.dot(wm, v.astype(bf16), preferred_element_type=f32) + bs_ref[:, h:h + 1]
            oa_ref[rows, h * HEAD_DIM:(h + 1) * HEAD_DIM] = (u * mixed).astype(oa_ref.dtype)


def _chunk_mlp(p, g, w, bs_rows, r, seq_rows, rb, out_dtype):
    m = p.shape[0]
    half = 2 * HEAD_DIM
    cu, cv = P_AU // half, P_AV // half
    kern = functools.partial(_chunk_mlp_kernel, r=r, seq_shift=int(math.log2(seq_rows)))
    return pl.pallas_call(
        kern,
        grid=(m // rb,),
        in_specs=[
            pl.BlockSpec((rb, half), lambda i: (i, cu)),
            pl.BlockSpec((rb, half), lambda i: (i, cu + 1)),
            pl.BlockSpec((rb, half), lambda i: (i, cv)),
            pl.BlockSpec((rb, half), lambda i: (i, cv + 1)),
            pl.BlockSpec((1, D_A), lambda i: (0, 0)),
            pl.BlockSpec((H_A, r, r), lambda i: (0, 0, 0)),
            pl.BlockSpec((r, LANES), lambda i: (0, 0)),
        ],
        out_specs=[
            pl.BlockSpec((rb, D_A), lambda i: (i, 0)),
            pl.BlockSpec((rb, D_A), lambda i: (i, 0)),
        ],
        out_shape=[jax.ShapeDtypeStruct((m, D_A), out_dtype), jax.ShapeDtypeStruct((m, D_A), f32)],
        compiler_params=_params("parallel"),
        name="chunk_mlp",
    )(p, p, p, p, g, w, bs_rows)


def _softplus(x):
    return jnp.maximum(x, 0.0) + jnp.log(1.0 + jnp.exp(-jnp.abs(x)))


def _mm(a, b):
    return jnp.dot(a.astype(bf16), b.astype(bf16), preferred_element_type=f32)


def _delta_kernel(x_ref, z_ref, ab_ref, hist_ref, s0_ref, cw_ref, alog_ref, dtb_ref, ng_ref,
                  o_ref, s_ref, cn_ref, ext, *, rows, t_valid, nchunks):
    c = CHUNK_B
    nsb = x_ref.shape[0]
    ci = pl.program_id(1)
    masked = t_valid < c
    rmask = lax.broadcasted_iota(jnp.int32, (c, 1), 0) < t_valid
    row = lax.broadcasted_iota(jnp.int32, (c, c), 0)
    col = lax.broadcasted_iota(jnp.int32, (c, c), 1)
    incl = row >= col
    strict = row > col
    eye = (row == col).astype(f32)
    blk8 = lax.shift_right_logical(row, 3) == lax.shift_right_logical(col, 3)
    lvl_masks = []
    for sh in range(3, int(math.log2(c))):
        rb = lax.shift_right_logical(row, sh)
        cb = lax.shift_right_logical(col, sh)
        lvl_masks.append(jnp.logical_and(jnp.bitwise_and(rb, 1) == 1, cb == rb - 1))
    ng = ng_ref[...]
    cw = cw_ref[...]

    @pl.when(ci == 0)
    def _():
        ext[:, 0:8, :] = jnp.zeros((nsb, 8, 3 * D_B), f32)
        ext[:, 5:8, :] = hist_ref[...]
        s_ref[...] = s0_ref[...]

    ys, zs, g_alls, beta_alls, gc_alls, egc_alls, gc_ts = [], [], [], [], [], [], []
    for sb in range(nsb):
        x = x_ref[sb]
        ext[sb, 8:8 + rows, :] = x
        y = (ext[sb, 5:5 + rows, :] * cw[0:1] + ext[sb, 6:6 + rows, :] * cw[1:2]
             + ext[sb, 7:7 + rows, :] * cw[2:3] + x * cw[3:4])

        @pl.when(ci == nchunks - 1)
        def _():
            cn_ref[sb] = ext[sb, 5 + t_valid:8 + t_valid, :]

        ext[sb, 0:8, :] = ext[sb, rows:rows + 8, :]

        y = y * _sigmoid(y)
        z = z_ref[sb]
        ab = ab_ref[sb]
        if rows < c:
            y = jnp.concatenate([y, jnp.zeros((c - rows, y.shape[1]), f32)], axis=0)
            z = jnp.concatenate([z, jnp.zeros((c - rows, z.shape[1]), f32)], axis=0)
            ab = jnp.concatenate([ab, jnp.zeros((c - rows, ab.shape[1]), f32)], axis=0)
        g_all = -jnp.exp(alog_ref[...]) * _softplus(ab + dtb_ref[...])
        beta_all = _sigmoid(ab)
        if masked:
            g_all = jnp.where(rmask, g_all, 0.0)
            beta_all = jnp.where(rmask, beta_all, 0.0)
        gc_all = jnp.dot(incl.astype(f32), g_all, precision=HI, preferred_element_type=f32)
        ys.append(y)
        zs.append(z)
        g_alls.append(g_all)
        beta_alls.append(beta_all)
        gc_alls.append(gc_all)
        egc_alls.append(jnp.exp(gc_all))
        gc_ts.append(jnp.transpose(jnp.concatenate([gc_all, jnp.zeros((LANES - c, LANES), f32)], axis=0)))

    hs = range(nsb * H_B)
    sq = lambda i: i // H_B
    hh = lambda i: i % H_B
    hd = lambda h, base=0: slice(base + h * HEAD_DIM, base + (h + 1) * HEAD_DIM)

    q = [ys[sq(i)][:, hd(hh(i))] for i in hs]
    k = [ys[sq(i)][:, hd(hh(i), D_B)] for i in hs]
    v = [ys[sq(i)][:, hd(hh(i), 2 * D_B)] for i in hs]
    q = [t * lax.rsqrt(jnp.sum(t * t, axis=-1, keepdims=True) + EPS) * (HEAD_DIM ** -0.5) for t in q]
    k = [t * lax.rsqrt(jnp.sum(t * t, axis=-1, keepdims=True) + EPS) for t in k]
    if masked:
        q = [jnp.where(rmask, t, 0.0) for t in q]
        k = [jnp.where(rmask, t, 0.0) for t in k]
        v = [jnp.where(rmask, t, 0.0) for t in v]
    gc = [gc_alls[sq(i)][:, hh(i):hh(i) + 1] for i in hs]
    egc = [egc_alls[sq(i)][:, hh(i):hh(i) + 1] for i in hs]
    beta = [beta_alls[sq(i)][:, H_B + hh(i):H_B + hh(i) + 1] for i in hs]
    decay = [jnp.exp(jnp.where(incl, gc[i] - gc_ts[sq(i)][hh(i):hh(i) + 1, 0:c], NEG_BIG)) for i in hs]
    kb = [k[h] * beta[h] for h in hs]
    vb = [v[h] * beta[h] for h in hs]
    k16 = [t.astype(bf16) for t in k]
    aq = [lax.dot_general(jnp.concatenate([kb[h], q[h]], axis=0).astype(bf16), k16[h], NT,
                          preferred_element_type=f32) for h in hs]
    a = [aq[h][:c] * jnp.where(strict, decay[h], 0.0) for h in hs]
    qk = [aq[h][c:] * decay[h] for h in hs]
    x = [jnp.where(blk8, -t, 0.0) for t in a]
    tinv = [eye + t for t in x]
    p1 = [_mm(t, t) for t in x]
    pt = [_mm(jnp.concatenate([p1[h], tinv[h]], axis=0), p1[h]) for h in hs]
    tinv = [tinv[h] + pt[h][c:] for h in hs]
    t2 = [_mm(tinv[h], pt[h][:c]) for h in hs]
    tinv = [tinv[h] + t2[h] for h in hs]
    for msk in lvl_masks:
        lt = [_mm(jnp.where(msk, a[h], 0.0), tinv[h]) for h in hs]
        tl = [_mm(tinv[h], lt[h]) for h in hs]
        tinv = [tinv[h] - tl[h] for h in hs]
    rhs = [jnp.concatenate([vb[h], kb[h] * egc[h]], axis=1) for h in hs]
    sol = [rhs[h] + _mm(tinv[h] - eye, rhs[h]) for h in hs]
    s = [s_ref[sq(i), hh(i)] for i in hs]
    wq = [_mm(jnp.concatenate([sol[h][:, HEAD_DIM:], q[h] * egc[h]], axis=0), s[h]) for h in hs]
    vn16 = [(sol[h][:, :HEAD_DIM] - wq[h][:c]).astype(bf16) for h in hs]
    o = [wq[h][c:] + jnp.dot(qk[h].astype(bf16), vn16[h], preferred_element_type=f32) for h in hs]
    g_last = [gc_alls[sq(i)][c - 1:c, hh(i):hh(i) + 1] for i in hs]
    kd = [(k[h] * jnp.exp(g_last[h] - gc[h])).astype(bf16) for h in hs]
    ds = [lax.dot_general(kd[h], vn16[h], TN, preferred_element_type=f32) for h in hs]
    for i in hs:
        s_ref[sq(i), hh(i)] = s[i] * jnp.exp(g_last[i]) + ds[i]
    for i in hs:
        zz = zs[sq(i)][:, hd(hh(i))]
        oh = _rms(o[i], ng) * (zz * _sigmoid(zz))
        o_ref[sq(i), :, hd(hh(i))] = oh[:rows].astype(o_ref.dtype)


def _delta(p, hist, s0, cw, alog, dtb, ng, nseq, rows, t_valid, nchunks, out_dtype):
    m = p.shape[0]
    nsb = DELTA_SEQS
    p3 = p.reshape(nseq, m // nseq, p.shape[1])
    kern = functools.partial(_delta_kernel, rows=rows, t_valid=t_valid, nchunks=nchunks)
    ob, s_new, conv_new = pl.pallas_call(
        kern,
        grid=(nseq // nsb, nchunks),
        in_specs=[
            pl.BlockSpec((nsb, rows, 3 * D_B), lambda b, c: (b, c, P_BQKV // (3 * D_B))),
            pl.BlockSpec((nsb, rows, D_B), lambda b, c: (b, c, P_BZ // D_B)),
            pl.BlockSpec((nsb, rows, LANES), lambda b, c: (b, c, P_AB // LANES)),
            pl.BlockSpec((nsb, CONV_B - 1, 3 * D_B), lambda b, c: (b, 0, 0)),
            pl.BlockSpec((nsb, H_B, HEAD_DIM, HEAD_DIM), lambda b, c: (b, 0, 0, 0)),
            pl.BlockSpec((CONV_B, 3 * D_B), lambda b, c: (0, 0)),
            pl.BlockSpec((1, LANES), lambda b, c: (0, 0)),
            pl.BlockSpec((1, LANES), lambda b, c: (0, 0)),
            pl.BlockSpec((1, HEAD_DIM), lambda b, c: (0, 0)),
        ],
        out_specs=[
            pl.BlockSpec((nsb, rows, D_B), lambda b, c: (b, c, 0)),
            pl.BlockSpec((nsb, H_B, HEAD_DIM, HEAD_DIM), lambda b, c: (b, 0, 0, 0)),
            pl.BlockSpec((nsb, CONV_B - 1, 3 * D_B), lambda b, c: (b, 0, 0)),
        ],
        out_shape=[
            jax.ShapeDtypeStruct((nseq, m // nseq, D_B), out_dtype),
            jax.ShapeDtypeStruct((nseq, H_B, HEAD_DIM, HEAD_DIM), f32),
            jax.ShapeDtypeStruct((nseq, CONV_B - 1, 3 * D_B), f32),
        ],
        scratch_shapes=[pltpu.VMEM((nsb, rows + 8, 3 * D_B), f32)],
        compiler_params=_params("parallel", "arbitrary"),
        name="delta",
    )(p3, p3, p3, hist, s0, cw, alog, dtb, ng)
    return ob.reshape(m, D_B), s_new, conv_new


def _lam(lam_ref, lam_init):
    lv = lam_ref[...]
    a = jnp.sum(lv[0:1] * lv[1:2], axis=-1, keepdims=True)
    b = jnp.sum(lv[2:3] * lv[3:4], axis=-1, keepdims=True)
    return jnp.exp(a) - jnp.exp(b) + lam_init


def _attn_kernel(lam_ref, q_ref, k_ref, v_ref, g_ref, o_ref, qs, m_s, l_s, acc, *, t, lam_init):
    qi = pl.program_id(2)
    q = q_ref[...] * (DC_HALF ** -0.5)
    lane = lax.broadcasted_iota(jnp.int32, q.shape, 1)
    qs[0] = jnp.where(lane < DC_HALF, q, 0.0).astype(bf16)
    qs[1] = jnp.where(lane >= DC_HALF, q, 0.0).astype(bf16)
    m_s[...] = jnp.full(m_s.shape, NEG_BIG, f32)
    l_s[...] = jnp.zeros(l_s.shape, f32)
    acc[...] = jnp.zeros(acc.shape, f32)

    def block(j, diag):
        off = pl.multiple_of(j * t, t)
        k = k_ref[pl.ds(off, t), :].astype(bf16)
        v = v_ref[pl.ds(off, t), :].astype(bf16)
        cs = range(2)
        s = [lax.dot_general(qs[c], k, NT, preferred_element_type=f32) for c in cs]
        if diag:
            row = lax.broadcasted_iota(jnp.int32, (t, t), 0)
            col = lax.broadcasted_iota(jnp.int32, (t, t), 1)
            s = [jnp.where(col <= row, sc, NEG_BIG) for sc in s]
        m_old = [m_s[c] for c in cs]
        m_new = [jnp.maximum(m_old[c], jnp.max(s[c], axis=-1, keepdims=True)) for c in cs]
        alpha = [jnp.exp(m_old[c] - m_new[c]) for c in cs]
        p = [jnp.exp(s[c] - jnp.concatenate([m_new[c]] * (t // HEAD_DIM), axis=1)) for c in cs]
        pv = [jnp.dot(p[c].astype(bf16), v, preferred_element_type=f32) for c in cs]
        for c in cs:
            l_s[c] = alpha[c] * l_s[c] + jnp.sum(p[c], axis=-1, keepdims=True)
            acc[c] = alpha[c] * acc[c] + pv[c]
            m_s[c] = m_new[c]

    def body(j, carry):
        block(j, False)
        return carry

    lax.fori_loop(0, qi, body, 0)
    block(qi, True)

    o = acc[0] / l_s[0] - _lam(lam_ref, lam_init) * (acc[1] / l_s[1])
    o = _rms(o, g_ref[...]) * (1.0 - lam_init)
    o_ref[...] = o.astype(o_ref.dtype)


def _attn(p, kbuf, vbuf, layer, lamv, g, nseq, seq, t, lam_init):
    m = p.shape[0]
    nb = seq // t
    cq = P_CQ // HEAD_DIM
    kern = functools.partial(_attn_kernel, t=t, lam_init=lam_init)
    kv_spec = pl.BlockSpec((None, None, None, seq, HEAD_DIM), lambda b, h, i: (b, layer, h, 0, 0))
    return pl.pallas_call(
        kern,
        grid=(nseq, H_C, nb),
        in_specs=[
            pl.BlockSpec((4, DC_HALF), lambda b, h, i: (0, 0)),
            pl.BlockSpec((t, HEAD_DIM), lambda b, h, i: (b * nb + i, cq + h)),
            kv_spec,
            kv_spec,
            pl.BlockSpec((1, HEAD_DIM), lambda b, h, i: (0, 0)),
        ],
        out_specs=pl.BlockSpec((t, HEAD_DIM), lambda b, h, i: (b * nb + i, h)),
        out_shape=jax.ShapeDtypeStruct((m, D_C), bf16),
        scratch_shapes=[
            pltpu.VMEM((2, t, HEAD_DIM), bf16),
            pltpu.VMEM((2, t, HEAD_DIM), f32),
            pltpu.VMEM((2, t, HEAD_DIM), f32),
            pltpu.VMEM((2, t, HEAD_DIM), f32),
        ],
        compiler_params=_params("parallel", "parallel", "arbitrary"),
        name="attn",
    )(lamv, p, kbuf, vbuf, g)


def _paged_attn_kernel(pt_ref, lam_ref, q0_ref, q1_ref, q2_ref, kn_ref, vn_ref, g_ref, *rest, npg, lam_init):
    k_refs = rest[:npg]
    v_refs = rest[npg:2 * npg]
    o_ref = rest[2 * npg]
    qs, kb, vb, m_s, l_s, acc = rest[2 * npg + 1:]
    gi = pl.program_id(1)
    hq = 2 * SAMPLE_ROWS

    @pl.when(gi == 0)
    def _():
        q = jnp.concatenate([q0_ref[...], q1_ref[...], q2_ref[...]], axis=1) * (DC_HALF ** -0.5)
        lane = lax.broadcasted_iota(jnp.int32, (SAMPLE_ROWS, HEAD_DIM), 1)
        for h in range(H_C):
            qh = q[:, h * HEAD_DIM:(h + 1) * HEAD_DIM]
            qs[h] = jnp.concatenate([jnp.where(lane < DC_HALF, qh, 0.0),
                                     jnp.where(lane >= DC_HALF, qh, 0.0)], axis=0).astype(bf16)
        m_s[...] = jnp.full(m_s.shape, NEG_BIG, f32)
        l_s[...] = jnp.zeros(l_s.shape, f32)
        acc[...] = jnp.zeros(acc.shape, f32)

    def update(s, v_of_head):
        m_old = m_s[...]
        m_new = jnp.maximum(m_old, jnp.max(s, axis=-1, keepdims=True))
        alpha = jnp.exp(m_old - m_new)
        p = jnp.exp(s - jnp.concatenate([m_new] * (s.shape[1] // LANES), axis=1))
        l_s[...] = alpha * l_s[...] + jnp.sum(p, axis=-1, keepdims=True)
        p16 = p.astype(bf16)
        pv = jnp.concatenate(
            [jnp.dot(p16[h * hq:(h + 1) * hq], v_of_head(h), preferred_element_type=f32) for h in range(H_C)],
            axis=0)
        acc[...] = alpha * acc[...] + pv
        m_s[...] = m_new

    for i in range(npg):
        for h in range(H_C):
            kb[h, i * PAGE_SIZE:(i + 1) * PAGE_SIZE, :] = k_refs[i][h].astype(bf16)
            vb[h, i * PAGE_SIZE:(i + 1) * PAGE_SIZE, :] = v_refs[i][h].astype(bf16)
    s_all = jnp.concatenate(
        [lax.dot_general(qs[h], kb[h], NT, preferred_element_type=f32) for h in range(H_C)], axis=0)
    update(s_all, lambda h: vb[h])

    @pl.when(gi == pl.num_programs(1) - 1)
    def _():
        pad = jnp.zeros((LANES - SAMPLE_ROWS, HEAD_DIM), f32)
        head = lambda x_ref, h: jnp.concatenate([x_ref[h], pad], axis=0).astype(bf16)
        s = jnp.concatenate(
            [lax.dot_general(qs[h], head(kn_ref, h), NT, preferred_element_type=f32) for h in range(H_C)], axis=0)
        row = lax.broadcasted_iota(jnp.int32, s.shape, 0)
        col = lax.broadcasted_iota(jnp.int32, s.shape, 1)
        s = jnp.where(col <= jnp.bitwise_and(row, SAMPLE_ROWS - 1), s, NEG_BIG)
        update(s, lambda h: head(vn_ref, h))
        o12 = acc[...] / l_s[...]
        lam = _lam(lam_ref, lam_init)
        for h in range(H_C):
            r1 = h * hq
            r2 = r1 + SAMPLE_ROWS
            o = o12[r1:r1 + SAMPLE_ROWS] - lam * o12[r2:r2 + SAMPLE_ROWS]
            o_ref[:, h * HEAD_DIM:(h + 1) * HEAD_DIM] = _rms(o, g_ref[...]) * (1.0 - lam_init)


def _paged_attn(p, kbuf, vbuf, lamv, g, cache_k, cache_v, page_table, layer, npg, lam_init):
    m = p.shape[0]
    nseq = m // SAMPLE_ROWS
    n_pages = page_table.shape[0] // nseq
    ngroups = n_pages // npg
    nq = 2 * H_C * SAMPLE_ROWS

    def page_spec(i):
        return pl.BlockSpec((None, None, H_C, PAGE_SIZE, HEAD_DIM),
                            lambda b, gq, pt: (pt[b * n_pages + gq * npg + i], layer, 0, 0, 0))

    new_spec = pl.BlockSpec((None, None, H_C, SAMPLE_ROWS, HEAD_DIM), lambda b, gq, pt: (b, layer, 0, 0, 0))
    q_spec = lambda c: pl.BlockSpec((SAMPLE_ROWS, 2 * HEAD_DIM), lambda b, gq, pt: (b, P_CQ // (2 * HEAD_DIM) + c))
    kern = functools.partial(_paged_attn_kernel, npg=npg, lam_init=lam_init)
    grid_spec = pltpu.PrefetchScalarGridSpec(
        num_scalar_prefetch=1,
        grid=(nseq, ngroups),
        in_specs=[
            pl.BlockSpec((4, DC_HALF), lambda b, gq, pt: (0, 0)),
            q_spec(0), q_spec(1), q_spec(2), new_spec, new_spec,
            pl.BlockSpec((1, HEAD_DIM), lambda b, gq, pt: (0, 0)),
        ] + [page_spec(i) for i in range(npg)] + [page_spec(i) for i in range(npg)],
        out_specs=pl.BlockSpec((SAMPLE_ROWS, D_C), lambda b, gq, pt: (b, 0)),
        scratch_shapes=[
            pltpu.VMEM((H_C, 2 * SAMPLE_ROWS, HEAD_DIM), bf16),
            pltpu.VMEM((H_C, npg * PAGE_SIZE, HEAD_DIM), bf16),
            pltpu.VMEM((H_C, npg * PAGE_SIZE, HEAD_DIM), bf16),
            pltpu.VMEM((nq, LANES), f32),
            pltpu.VMEM((nq, LANES), f32),
            pltpu.VMEM((nq, HEAD_DIM), f32),
        ],
    )
    return pl.pallas_call(
        kern,
        grid_spec=grid_spec,
        out_shape=jax.ShapeDtypeStruct((m, D_C), f32),
        compiler_params=_params("parallel", "arbitrary"),
        name="paged_attn",
    )(page_table, lamv, p, p, p, kbuf, vbuf, g, *([cache_k] * npg), *([cache_v] * npg))


def _out_proj_kernel(a_ref, b_ref, c_ref, x_ref, w_ref, g_ref, xo_ref, h_ref):
    mix = jnp.concatenate([a_ref[...].astype(bf16), b_ref[...].astype(bf16), c_ref[...].astype(bf16)], axis=1)
    for n in range(D_MODEL // MXU_WIDTH):
        cols = slice(n * MXU_WIDTH, (n + 1) * MXU_WIDTH)
        xo_ref[:, cols] = x_ref[:, cols] + jnp.dot(mix, w_ref[:, cols], preferred_element_type=f32)
    h_ref[...] = _rms(xo_ref[...], g_ref[...]).astype(bf16)


def _out_proj(a, b, c, x, wo, layer, g, tm):
    m = x.shape[0]
    row = lambda w: pl.BlockSpec((tm, w), lambda i: (i, 0))
    return pl.pallas_call(
        _out_proj_kernel,
        grid=(m // tm,),
        in_specs=[
            row(D_A), row(D_B), row(D_C), row(D_MODEL),
            pl.BlockSpec((None, D_MODEL, D_MODEL), lambda i: (layer, 0, 0), pipeline_mode=pl.Buffered(1)),
            pl.BlockSpec((1, D_MODEL), lambda i: (0, 0)),
        ],
        out_specs=[row(D_MODEL), row(D_MODEL)],
        out_shape=[jax.ShapeDtypeStruct((m, D_MODEL), f32), jax.ShapeDtypeStruct((m, D_MODEL), bf16)],
        compiler_params=_params("parallel"),
        name="out_proj",
    )(a, b, c, x, wo, g)


def _ffn_up_kernel(h_ref, wg_ref, wv_ref, cwg_ref, cwv_ref, cbg_ref, cbv_ref, hg_ref, hv_ref,
                   act_ref, ng_ref, nv_ref, *, nseq, seq_rows, t_valid):
    h = h_ref[...]
    tf = wg_ref.shape[1]
    ts = min(tf, FFN_SUB)
    rin = lax.broadcasted_iota(jnp.int32, (nseq, seq_rows, ts), 1)

    def conv(w_ref, cw_ref, cb_ref, hist_ref, new_ref, cols):
        w = w_ref[:, cols]
        rb = min(h.shape[0], FFN_ROWS)
        up = jnp.concatenate([jnp.dot(h[r:r + rb], w, preferred_element_type=f32)
                              for r in range(0, h.shape[0], rb)], axis=0)
        up3 = up.reshape(nseq, seq_rows, ts)
        r1 = pltpu.roll(up, 1, axis=0).reshape(nseq, seq_rows, ts)
        r2 = pltpu.roll(up, 2, axis=0).reshape(nseq, seq_rows, ts)
        hist = hist_ref[:, :, cols]
        h0 = hist[:, 0:1, :]
        h1 = hist[:, 1:2, :]
        prev1 = jnp.where(rin == 0, h1, r1)
        prev2 = jnp.where(rin == 0, h0, jnp.where(rin == 1, h1, r2))
        cw = cw_ref[:, cols]
        y = prev2 * cw[0:1] + prev1 * cw[1:2] + up3 * cw[2:3] + cb_ref[:, cols]
        new_ref[:, :, cols] = up3[:, t_valid - 2:t_valid, :]
        return y

    for sub in range(tf // ts):
        cols = slice(sub * ts, (sub + 1) * ts)
        yg = conv(wg_ref, cwg_ref, cbg_ref, hg_ref, ng_ref, cols)
        yv = conv(wv_ref, cwv_ref, cbv_ref, hv_ref, nv_ref, cols)
        act = yg * _sigmoid(yg) * yv
        act_ref[:, cols] = act.reshape(nseq * seq_rows, ts).astype(bf16)


def _ffn_up(h2, w_up, layer, cw, cb, hist, nseq_blk, seq_rows, t_valid, tf):
    m = h2.shape[0]
    tm = nseq_blk * seq_rows
    nseq = m // seq_rows
    nj = D_FF // tf
    kern = functools.partial(_ffn_up_kernel, nseq=nseq_blk, seq_rows=seq_rows, t_valid=t_valid)
    gate = lambda i, j: (0, j)
    val = lambda i, j: (0, nj + j)
    return pl.pallas_call(
        kern,
        grid=(m // tm, nj),
        in_specs=[
            pl.BlockSpec((tm, D_MODEL), lambda i, j: (i, 0)),
            pl.BlockSpec((None, D_MODEL, tf), lambda i, j: (layer, 0, j)),
            pl.BlockSpec((None, D_MODEL, tf), lambda i, j: (layer, 0, nj + j)),
            pl.BlockSpec((CONV_FFN, tf), gate),
            pl.BlockSpec((CONV_FFN, tf), val),
            pl.BlockSpec((1, tf), gate),
            pl.BlockSpec((1, tf), val),
            pl.BlockSpec((nseq_blk, CONV_FFN - 1, tf), lambda i, j: (i, 0, j)),
            pl.BlockSpec((nseq_blk, CONV_FFN - 1, tf), lambda i, j: (i, 0, nj + j)),
        ],
        out_specs=[
            pl.BlockSpec((tm, tf), lambda i, j: (i, j)),
            pl.BlockSpec((nseq_blk, CONV_FFN - 1, tf), lambda i, j: (i, 0, j)),
            pl.BlockSpec((nseq_blk, CONV_FFN - 1, tf), lambda i, j: (i, 0, j)),
        ],
        out_shape=[
            jax.ShapeDtypeStruct((m, D_FF), bf16),
            jax.ShapeDtypeStruct((nseq, CONV_FFN - 1, D_FF), f32),
            jax.ShapeDtypeStruct((nseq, CONV_FFN - 1, D_FF), f32),
        ],
        compiler_params=_params("parallel", "arbitrary"),
        name="ffn_up",
    )(h2, w_up, w_up, cw, cw, cb, cb, hist, hist)


def _ffn_down_kernel(a_ref, w_ref, x_ref, g_ref, o_ref, *, final_norm):
    a = a_ref[...]
    for n in range(D_MODEL // MXU_WIDTH):
        cols = slice(n * MXU_WIDTH, (n + 1) * MXU_WIDTH)
        o_ref[:, cols] = x_ref[:, cols] + jnp.dot(a, w_ref[:, cols], preferred_element_type=f32)
    if final_norm:
        o_ref[...] = _rms(o_ref[...], g_ref[...])


def _ffn_down(act, w_down, layer, x, g, tm, final_norm):
    m = x.shape[0]
    kern = functools.partial(_ffn_down_kernel, final_norm=final_norm)
    return pl.pallas_call(
        kern,
        grid=(m // tm,),
        in_specs=[
            pl.BlockSpec((tm, D_FF), lambda i: (i, 0)),
            pl.BlockSpec((None, D_FF, D_MODEL), lambda i: (layer, 0, 0), pipeline_mode=pl.Buffered(1)),
            pl.BlockSpec((tm, D_MODEL), lambda i: (i, 0)),
            pl.BlockSpec((1, D_MODEL), lambda i: (0, 0)),
        ],
        out_specs=pl.BlockSpec((tm, D_MODEL), lambda i: (i, 0)),
        out_shape=jax.ShapeDtypeStruct((m, D_MODEL), f32),
        compiler_params=_params("parallel"),
        name="ffn_down",
    )(act, w_down, x, g)


def _pad_lanes(v, offset=0):
    return jnp.zeros((1, LANES), f32).at[0, offset:offset + v.shape[0]].set(v)


def _prep_weights(w):
    wt = jnp.swapaxes(w["w_in"], 1, 2).astype(bf16)
    stacked = dict(
        w_in_t=wt,
        w_q=jnp.concatenate([wt[:, OFF_CQ:OFF_CK], wt[:, OFF_BA:OFF_CQ],
                             jnp.zeros((DEPTH, N_MAIN - P_AB - 2 * H_B, D_MODEL), bf16)], axis=1),
        w_kv=wt[:, OFF_CK:N_IN],
        w_out=w["w_out"].astype(bf16), w_up=w["w_up"].astype(bf16), w_down=w["w_down"].astype(bf16))
    out = []
    for l in range(DEPTH):
        out.append(dict(
            norm1_g=w["norm1_g"][l][None],
            a_norm_g=w["a_norm_g"][l].reshape(1, D_A),
            a_ws=w["a_ws"][l], a_bs=w["a_bs"][l],
            b_conv_w=w["b_conv_w"][l],
            alog=_pad_lanes(w["b_a_log"][l]), dtb=_pad_lanes(w["b_dt_bias"][l]),
            b_norm_g=w["b_norm_g"][l][None],
            lamv=jnp.stack([w["c_lam_q1"][l], w["c_lam_k1"][l], w["c_lam_q2"][l], w["c_lam_k2"][l]]),
            c_norm_g=w["c_norm_g"][l][None],
            norm2_g=w["norm2_g"][l][None],
            ffn_conv_w=w["ffn_conv_w"][l], ffn_conv_b=w["ffn_conv_b"][l][None],
        ))
    return out, stacked


def _blocks(m):
    return dict(
        in_rows=min(2 * ACC_ROWS, m),
        rows=min(ACC_ROWS, m),
        ffn_cols=2 * MXU_WIDTH,
        attn=ACC_ROWS,
    )


def _run_trunk(x, lw, sw, final_g, nseq, seq_rows, t_valid, past):
    m = x.shape[0]
    sample = past is not None
    blk = _blocks(m)
    tm = blk["rows"]
    kbuf = jnp.zeros((nseq, DEPTH, H_C, seq_rows, HEAD_DIM), f32)
    vbuf = jnp.zeros((nseq, DEPTH, H_C, seq_rows, HEAD_DIM), f32)
    deltas, convs, ffns, chunk_vs = [], [], [], []
    for l in range(DEPTH):
        w = lw[l]
        lam_init = 0.8 - 0.6 * math.exp(-0.3 * l)
        p, h = _in_proj(x, w["norm1_g"], sw["w_in_t"], sw["w_q"], l, blk["in_rows"])
        kbuf, vbuf = _kv_proj(h, sw["w_kv"], kbuf, vbuf, l, seq_rows, tm)
        if sample:
            a_w = jnp.tile(w["a_ws"][:, :seq_rows, :seq_rows], (1, nseq, nseq))
            a_b = jnp.tile(w["a_bs"][:, :seq_rows].T, (nseq, 1))
            oa, va = _chunk_mlp(p, w["a_norm_g"], a_w, jnp.pad(a_b, ((0, 0), (0, LANES - H_A))),
                                m, seq_rows, m, f32)
            ob, s_new, conv_new = _delta(
                p, past["state_conv_qkv"][:, l], past["state_delta"][:, l], w["b_conv_w"], w["alog"],
                w["dtb"], w["b_norm_g"], nseq, seq_rows, t_valid, 1, f32)
            oc = _paged_attn(p, kbuf, vbuf, w["lamv"], w["c_norm_g"], past["cache_k"], past["cache_v"],
                             past["page_table"], l, PAGES_PER_STEP, lam_init)
            ffn_hist = past["state_ffn_conv"][:, l]
            nseq_blk = nseq
        else:
            a_b = jnp.pad(w["a_bs"].T, ((0, 0), (0, LANES - H_A)))
            oa, va = _chunk_mlp(p, w["a_norm_g"], w["a_ws"], a_b, CHUNK_A, CHUNK_A, tm, bf16)
            ob, s_new, conv_new = _delta(
                p, jnp.zeros((nseq, CONV_B - 1, 3 * D_B), f32),
                jnp.zeros((nseq, H_B, HEAD_DIM, HEAD_DIM), f32), w["b_conv_w"], w["alog"], w["dtb"],
                w["b_norm_g"], nseq, CHUNK_B, CHUNK_B, seq_rows // CHUNK_B, bf16)
            oc = _attn(p, kbuf, vbuf, l, w["lamv"], w["c_norm_g"], nseq, seq_rows,
                       min(blk["attn"], seq_rows), lam_init)
            ffn_hist = jnp.zeros((nseq, CONV_FFN - 1, 2 * D_FF), f32)
            nseq_blk = 1
        x, h2 = _out_proj(oa, ob, oc, x, sw["w_out"], l, w["norm2_g"], tm)
        act, hg, hv = _ffn_up(h2, sw["w_up"], l, w["ffn_conv_w"], w["ffn_conv_b"], ffn_hist,
                              nseq_blk, seq_rows, t_valid, blk["ffn_cols"])
        x = _ffn_down(act, sw["w_down"], l, x, final_g[None], tm, l == DEPTH - 1)
        deltas.append(s_new)
        convs.append(conv_new)
        ffns.append(jnp.concatenate([hg, hv], axis=-1))
        if sample:
            chunk_vs.append(va.reshape(nseq, seq_rows, H_A, HEAD_DIM)[:, :t_valid])
    y = x.reshape(nseq, seq_rows, D_MODEL)[:, :t_valid]
    k_rows = jnp.transpose(kbuf, (0, 1, 3, 2, 4))[:, :, :t_valid]
    v_rows = jnp.transpose(vbuf, (0, 1, 3, 2, 4))[:, :, :t_valid]
    stack = lambda xs: jnp.stack(xs, axis=1)
    deltas = jnp.transpose(jnp.stack(deltas, axis=0), (1, 0, 2, 3, 4))
    convs = jnp.transpose(jnp.stack(convs, axis=2), (0, 2, 1, 3))
    return (y, k_rows, v_rows, deltas, convs, stack(ffns), stack(chunk_vs) if sample else None)


def kernel(x_prompt, x_sample, cache_k, cache_v, page_table, state_delta, state_conv_qkv, state_ffn_conv,
           norm1_g, w_in, a_norm_g, a_ws, a_bs, b_conv_w, b_a_log, b_dt_bias, b_norm_g, c_lam_q1, c_lam_k1,
           c_lam_q2, c_lam_k2, c_norm_g, w_out, norm2_g, w_up, ffn_conv_w, ffn_conv_b, w_down, final_g):
    weights = dict(norm1_g=norm1_g, w_in=w_in, a_norm_g=a_norm_g, a_ws=a_ws, a_bs=a_bs, b_conv_w=b_conv_w,
                   b_a_log=b_a_log, b_dt_bias=b_dt_bias, b_norm_g=b_norm_g, c_lam_q1=c_lam_q1,
                   c_lam_k1=c_lam_k1, c_lam_q2=c_lam_q2, c_lam_k2=c_lam_k2, c_norm_g=c_norm_g, w_out=w_out,
                   norm2_g=norm2_g, w_up=w_up, ffn_conv_w=ffn_conv_w, ffn_conv_b=ffn_conv_b, w_down=w_down)
    lw, sw = _prep_weights(weights)

    batch, seq, _ = x_prompt.shape
    yp, pk, pv, pd, pc, pf, _ = _run_trunk(
        x_prompt.reshape(batch * seq, D_MODEL), lw, sw, final_g, batch, seq, seq, None)

    dbatch, dseq, _ = x_sample.shape
    xs = jnp.pad(x_sample, ((0, 0), (0, SAMPLE_ROWS - dseq), (0, 0))).reshape(dbatch * SAMPLE_ROWS, D_MODEL)
    past = dict(
        cache_k=jnp.transpose(cache_k, (0, 1, 3, 2, 4)),
        cache_v=jnp.transpose(cache_v, (0, 1, 3, 2, 4)),
        page_table=page_table.reshape(-1).astype(jnp.int32),
        state_delta=state_delta, state_conv_qkv=state_conv_qkv, state_ffn_conv=state_ffn_conv)
    ys, sk, sv, sd, sc, sf, scv = _run_trunk(xs, lw, sw, final_g, dbatch, SAMPLE_ROWS, dseq, past)
    return (yp, ys, pk, pv, pd, pc, pf, sk, sv, sd, sc, sf, scv)
```

```python
import functools
import math

import jax
import jax.numpy as jnp
from jax import lax
from jax.experimental import pallas as pl
from jax.experimental.pallas import tpu as pltpu

f32 = jnp.float32
bf16 = jnp.bfloat16

D_MODEL = 2048
DEPTH = 4
HEAD_DIM = 128
H_A = 4
D_A = H_A * HEAD_DIM
CHUNK_A = 128
H_B = 6
D_B = H_B * HEAD_DIM
H_C = 6
D_C = H_C * HEAD_DIM
DC_HALF = HEAD_DIM // 2
CONV_B = 4
CHUNK_B = 64
D_FF = 5632
CONV_FFN = 3
EPS = 1e-6
PAGE_SIZE = 128
SAMPLE_ROWS = 8
DELTA_SEQS = 4
PAGES_PER_STEP = 16

OFF_AU = 0
OFF_AV = OFF_AU + D_A
OFF_BQKV = OFF_AV + D_A
OFF_BZ = OFF_BQKV + 3 * D_B
OFF_BA = OFF_BZ + D_B
OFF_BB = OFF_BA + H_B
OFF_CQ = OFF_BB + H_B
OFF_CK = OFF_CQ + D_C
OFF_CV = OFF_CK + D_C
N_IN = OFF_CV + D_C

IN_COLS = 1024
P_BQKV = 0
P_BZ = P_BQKV + 3 * D_B
P_AU = P_BZ + D_B
P_AV = P_AU + D_A
P_CQ = P_AV + D_A
P_AB = P_CQ + D_C
N_MAIN = P_CQ + IN_COLS
N_KV = 2 * D_C
assert OFF_BQKV == IN_COLS and OFF_BA - OFF_BQKV == 3 * IN_COLS and P_AU == 3 * IN_COLS
LANES = 128

MXU_WIDTH = 256
FFN_SUB = MXU_WIDTH
ACC_ROWS = 512
FFN_ROWS = ACC_ROWS
VMEM_LIMIT = 56 * 1024 * 1024
NEG_BIG = -1e30
HI = lax.Precision.HIGHEST
NT = (((1,), (1,)), ((), ()))
TN = (((0,), (0,)), ((), ()))


def _params(*sem):
    return pltpu.CompilerParams(dimension_semantics=sem, vmem_limit_bytes=VMEM_LIMIT)


def _sigmoid(x):
    return 1.0 / (1.0 + jnp.exp(-x))


def _gelu_tanh(x):
    c = math.sqrt(2.0 / math.pi)
    return 0.5 * x * (1.0 + jnp.tanh(c * (x + 0.044715 * (x * x * x))))


def _rms(x, g):
    return x * lax.rsqrt(jnp.mean(x * x, axis=-1, keepdims=True) + EPS) * g


def _in_proj_kernel(x_ref, g_ref, w_ref, wq_ref, p_ref, h_ref):
    j = pl.program_id(1)

    @pl.when(j == 0)
    def _():
        h_ref[...] = _rms(x_ref[...], g_ref[...]).astype(bf16)

    tm, tn = p_ref.shape
    rb = min(tm, ACC_ROWS)

    def project(src_ref):
        for n in range(tn // MXU_WIDTH):
            cols = slice(n * MXU_WIDTH, (n + 1) * MXU_WIDTH)
            for r in range(0, tm, rb):
                p_ref[r:r + rb, cols] = lax.dot_general(h_ref[r:r + rb, :], src_ref[cols, :], NT,
                                                        preferred_element_type=f32)

    last = pl.num_programs(1) - 1

    @pl.when(j < last)
    def _():
        project(w_ref)

    @pl.when(j == last)
    def _():
        project(wq_ref)


def _in_proj(x, g, wt, wq, layer, tm):
    m = x.shape[0]
    nsteps = N_MAIN // IN_COLS
    src_block = lambda j: jnp.where(j < 3, j + 1, 0)
    return pl.pallas_call(
        _in_proj_kernel,
        grid=(m // tm, nsteps),
        in_specs=[
            pl.BlockSpec((tm, D_MODEL), lambda i, j: (i, 0)),
            pl.BlockSpec((1, D_MODEL), lambda i, j: (0, 0)),
            pl.BlockSpec((None, IN_COLS, D_MODEL), lambda i, j: (layer, src_block(j), 0)),
            pl.BlockSpec((None, IN_COLS, D_MODEL), lambda i, j: (layer, 0, 0)),
        ],
        out_specs=[
            pl.BlockSpec((tm, IN_COLS), lambda i, j: (i, j)),
            pl.BlockSpec((tm, D_MODEL), lambda i, j: (i, 0)),
        ],
        out_shape=[jax.ShapeDtypeStruct((m, N_MAIN), f32), jax.ShapeDtypeStruct((m, D_MODEL), bf16)],
        compiler_params=_params("parallel", "arbitrary"),
        name="in_proj",
    )(x, g, wt, wq)


def _kv_proj_kernel(h_ref, w_ref, kin_ref, vin_ref, k_ref, v_ref):
    del kin_ref, vin_ref
    h = h_ref[...]
    for n in range(N_KV // MXU_WIDTH):
        res = lax.dot_general(h, w_ref[n * MXU_WIDTH:(n + 1) * MXU_WIDTH, :], NT, preferred_element_type=f32)
        for half in range(MXU_WIDTH // HEAD_DIM):
            head = n * (MXU_WIDTH // HEAD_DIM) + half
            dst = k_ref if head < H_C else v_ref
            piece = res[:, half * HEAD_DIM:(half + 1) * HEAD_DIM]
            if len(dst.shape) == 3:
                dst[head % H_C] = piece
            else:
                rows = dst.shape[2]
                for b in range(dst.shape[0]):
                    dst[b, head % H_C] = piece[b * rows:(b + 1) * rows]


def _kv_proj(h, w_kv, kbuf, vbuf, layer, seq_rows, tm):
    m = h.shape[0]
    if tm <= seq_rows:
        per_seq = seq_rows // tm
        buf_spec = pl.BlockSpec((None, None, H_C, tm, HEAD_DIM),
                                lambda i: (i // per_seq, layer, 0, i % per_seq, 0))
    else:
        buf_spec = pl.BlockSpec((tm // seq_rows, None, H_C, seq_rows, HEAD_DIM), lambda i: (i, layer, 0, 0, 0))
    return pl.pallas_call(
        _kv_proj_kernel,
        grid=(m // tm,),
        in_specs=[
            pl.BlockSpec((tm, D_MODEL), lambda i: (i, 0)),
            pl.BlockSpec((None, N_KV, D_MODEL), lambda i: (layer, 0, 0), pipeline_mode=pl.Buffered(1)),
            pl.BlockSpec(memory_space=pl.ANY),
            pl.BlockSpec(memory_space=pl.ANY),
        ],
        out_specs=[buf_spec, buf_spec],
        out_shape=[jax.ShapeDtypeStruct(kbuf.shape, f32), jax.ShapeDtypeStruct(vbuf.shape, f32)],
        input_output_aliases={2: 0, 3: 1},
        compiler_params=_params("parallel"),
        name="kv_proj",
    )(h, w_kv, kbuf, vbuf)


def _chunk_mlp_kernel(u0_ref, u1_ref, v0_ref, v1_ref, g_ref, w_ref, bs_ref, oa_ref, va_ref, *, r, seq_shift):
    nsub = u0_ref.shape[0] // r
    row = lax.broadcasted_iota(jnp.int32, (r, r), 0)
    col = lax.broadcasted_iota(jnp.int32, (r, r), 1)
    same_seq = lax.shift_right_logical(row, seq_shift) == lax.shift_right_logical(col, seq_shift)
    mask = jnp.logical_and(same_seq, col <= row)
    g = g_ref[...]
    for h in range(H_A):
        wm = jnp.where(mask, w_ref[h], 0.0).astype(bf16)
        u_ref = (u0_ref, u1_ref)[h // 2]
        v_ref = (v0_ref, v1_ref)[h // 2]
        lo = (h % 2) * HEAD_DIM
        for c in range(nsub):
            rows = slice(c * r, (c + 1) * r)
            u = _gelu_tanh(u_ref[rows, lo:lo + HEAD_DIM])
            v = _rms(_gelu_tanh(v_ref[rows, lo:lo + HEAD_DIM]), g[:, h * HEAD_DIM:(h + 1) * HEAD_DIM])
            va_ref[rows, h * HEAD_DIM:(h + 1) * HEAD_DIM] = v
            mixed = jnp.dot(wm, v.astype(bf16), preferred_element_type=f32) + bs_ref[:, h:h + 1]
            oa_ref[rows, h * HEAD_DIM:(h + 1) * HEAD_DIM] = (u * mixed).astype(oa_ref.dtype)


def _chunk_mlp(p, g, w, bs_rows, r, seq_rows, rb, out_dtype):
    m = p.shape[0]
    half = 2 * HEAD_DIM
    cu, cv = P_AU // half, P_AV // half
    kern = functools.partial(_chunk_mlp_kernel, r=r, seq_shift=int(math.log2(seq_rows)))
    return pl.pallas_call(
        kern,
        grid=(m // rb,),
        in_specs=[
            pl.BlockSpec((rb, half), lambda i: (i, cu)),
            pl.BlockSpec((rb, half), lambda i: (i, cu + 1)),
            pl.BlockSpec((rb, half), lambda i: (i, cv)),
            pl.BlockSpec((rb, half), lambda i: (i, cv + 1)),
            pl.BlockSpec((1, D_A), lambda i: (0, 0)),
            pl.BlockSpec((H_A, r, r), lambda i: (0, 0, 0)),
            pl.BlockSpec((r, LANES), lambda i: (0, 0)),
        ],
        out_specs=[
            pl.BlockSpec((rb, D_A), lambda i: (i, 0)),
            pl.BlockSpec((rb, D_A), lambda i: (i, 0)),
        ],
        out_shape=[jax.ShapeDtypeStruct((m, D_A), out_dtype), jax.ShapeDtypeStruct((m, D_A), f32)],
        compiler_params=_params("parallel"),
        name="chunk_mlp",
    )(p, p, p, p, g, w, bs_rows)


def _softplus(x):
    return jnp.maximum(x, 0.0) + jnp.log(1.0 + jnp.exp(-jnp.abs(x)))


def _mm(a, b):
    return jnp.dot(a.astype(bf16), b.astype(bf16), preferred_element_type=f32)


def _delta_kernel(x_ref, z_ref, ab_ref, hist_ref, s0_ref, cw_ref, alog_ref, dtb_ref, ng_ref,
                  o_ref, s_ref, cn_ref, ext, *, rows, t_valid, nchunks):
    c = CHUNK_B
    nsb = x_ref.shape[0]
    ci = pl.program_id(1)
    masked = t_valid < c
    rmask = lax.broadcasted_iota(jnp.int32, (c, 1), 0) < t_valid
    row = lax.broadcasted_iota(jnp.int32, (c, c), 0)
    col = lax.broadcasted_iota(jnp.int32, (c, c), 1)
    incl = row >= col
    strict = row > col
    eye = (row == col).astype(f32)
    blk8 = lax.shift_right_logical(row, 3) == lax.shift_right_logical(col, 3)
    lvl_masks = []
    for sh in range(3, int(math.log2(c))):
        rb = lax.shift_right_logical(row, sh)
        cb = lax.shift_right_logical(col, sh)
        lvl_masks.append(jnp.logical_and(jnp.bitwise_and(rb, 1) == 1, cb == rb - 1))
    ng = ng_ref[...]
    cw = cw_ref[...]

    @pl.when(ci == 0)
    def _():
        ext[:, 0:8, :] = jnp.zeros((nsb, 8, 3 * D_B), f32)
        ext[:, 5:8, :] = hist_ref[...]
        s_ref[...] = s0_ref[...]

    ys, zs, g_alls, beta_alls, gc_alls, egc_alls, gc_ts = [], [], [], [], [], [], []
    for sb in range(nsb):
        x = x_ref[sb]
        ext[sb, 8:8 + rows, :] = x
        y = (ext[sb, 5:5 + rows, :] * cw[0:1] + ext[sb, 6:6 + rows, :] * cw[1:2]
             + ext[sb, 7:7 + rows, :] * cw[2:3] + x * cw[3:4])

        @pl.when(ci == nchunks - 1)
        def _():
            cn_ref[sb] = ext[sb, 5 + t_valid:8 + t_valid, :]

        ext[sb, 0:8, :] = ext[sb, rows:rows + 8, :]

        y = y * _sigmoid(y)
        z = z_ref[sb]
        ab = ab_ref[sb]
        if rows < c:
            y = jnp.concatenate([y, jnp.zeros((c - rows, y.shape[1]), f32)], axis=0)
            z = jnp.concatenate([z, jnp.zeros((c - rows, z.shape[1]), f32)], axis=0)
            ab = jnp.concatenate([ab, jnp.zeros((c - rows, ab.shape[1]), f32)], axis=0)
        g_all = -jnp.exp(alog_ref[...]) * _softplus(ab + dtb_ref[...])
        beta_all = _sigmoid(ab)
        if masked:
            g_all = jnp.where(rmask, g_all, 0.0)
            beta_all = jnp.where(rmask, beta_all, 0.0)
        gc_all = jnp.dot(incl.astype(f32), g_all, precision=HI, preferred_element_type=f32)
        ys.append(y)
        zs.append(z)
        g_alls.append(g_all)
        beta_alls.append(beta_all)
        gc_alls.append(gc_all)
        egc_alls.append(jnp.exp(gc_all))
        gc_ts.append(jnp.transpose(jnp.concatenate([gc_all, jnp.zeros((LANES - c, LANES), f32)], axis=0)))

    hs = range(nsb * H_B)
    sq = lambda i: i // H_B
    hh = lambda i: i % H_B
    hd = lambda h, base=0: slice(base + h * HEAD_DIM, base + (h + 1) * HEAD_DIM)

    q = [ys[sq(i)][:, hd(hh(i))] for i in hs]
    k = [ys[sq(i)][:, hd(hh(i), D_B)] for i in hs]
    v = [ys[sq(i)][:, hd(hh(i), 2 * D_B)] for i in hs]
    q = [t * lax.rsqrt(jnp.sum(t * t, axis=-1, keepdims=True) + EPS) * (HEAD_DIM ** -0.5) for t in q]
    k = [t * lax.rsqrt(jnp.sum(t * t, axis=-1, keepdims=True) + EPS) for t in k]
    if masked:
        q = [jnp.where(rmask, t, 0.0) for t in q]
        k = [jnp.where(rmask, t, 0.0) for t in k]
        v = [jnp.where(rmask, t, 0.0) for t in v]
    gc = [gc_alls[sq(i)][:, hh(i):hh(i) + 1] for i in hs]
    egc = [egc_alls[sq(i)][:, hh(i):hh(i) + 1] for i in hs]
    beta = [beta_alls[sq(i)][:, H_B + hh(i):H_B + hh(i) + 1] for i in hs]
    decay = [jnp.exp(jnp.where(incl, gc[i] - gc_ts[sq(i)][hh(i):hh(i) + 1, 0:c], NEG_BIG)) for i in hs]
    kb = [k[h] * beta[h] for h in hs]
    vb = [v[h] * beta[h] for h in hs]
    k16 = [t.astype(bf16) for t in k]
    aq = [lax.dot_general(jnp.concatenate([kb[h], q[h]], axis=0).astype(bf16), k16[h], NT,
                          preferred_element_type=f32) for h in hs]
    a = [aq[h][:c] * jnp.where(strict, decay[h], 0.0) for h in hs]
    qk = [aq[h][c:] * decay[h] for h in hs]
    x = [jnp.where(blk8, -t, 0.0) for t in a]
    tinv = [eye + t for t in x]
    p1 = [_mm(t, t) for t in x]
    pt = [_mm(jnp.concatenate([p1[h], tinv[h]], axis=0), p1[h]) for h in hs]
    tinv = [tinv[h] + pt[h][c:] for h in hs]
    t2 = [_mm(tinv[h], pt[h][:c]) for h in hs]
    tinv = [tinv[h] + t2[h] for h in hs]
    for msk in lvl_masks:
        lt = [_mm(jnp.where(msk, a[h], 0.0), tinv[h]) for h in hs]
        tl = [_mm(tinv[h], lt[h]) for h in hs]
        tinv = [tinv[h] - tl[h] for h in hs]
    rhs = [jnp.concatenate([vb[h], kb[h] * egc[h]], axis=1) for h in hs]
    sol = [rhs[h] + _mm(tinv[h] - eye, rhs[h]) for h in hs]
    s = [s_ref[sq(i), hh(i)] for i in hs]
    wq = [_mm(jnp.concatenate([sol[h][:, HEAD_DIM:], q[h] * egc[h]], axis=0), s[h]) for h in hs]
    vn16 = [(sol[h][:, :HEAD_DIM] - wq[h][:c]).astype(bf16) for h in hs]
    o = [wq[h][c:] + jnp.dot(qk[h].astype(bf16), vn16[h], preferred_element_type=f32) for h in hs]
    g_last = [gc_alls[sq(i)][c - 1:c, hh(i):hh(i) + 1] for i in hs]
    kd = [(k[h] * jnp.exp(g_last[h] - gc[h])).astype(bf16) for h in hs]
    ds = [lax.dot_general(kd[h], vn16[h], TN, preferred_element_type=f32) for h in hs]
    for i in hs:
        s_ref[sq(i), hh(i)] = s[i] * jnp.exp(g_last[i]) + ds[i]
    for i in hs:
        zz = zs[sq(i)][:, hd(hh(i))]
        oh = _rms(o[i], ng) * (zz * _sigmoid(zz))
        o_ref[sq(i), :, hd(hh(i))] = oh[:rows].astype(o_ref.dtype)


def _delta(p, hist, s0, cw, alog, dtb, ng, nseq, rows, t_valid, nchunks, out_dtype):
    m = p.shape[0]
    nsb = DELTA_SEQS
    p3 = p.reshape(nseq, m // nseq, p.shape[1])
    kern = functools.partial(_delta_kernel, rows=rows, t_valid=t_valid, nchunks=nchunks)
    ob, s_new, conv_new = pl.pallas_call(
        kern,
        grid=(nseq // nsb, nchunks),
        in_specs=[
            pl.BlockSpec((nsb, rows, 3 * D_B), lambda b, c: (b, c, P_BQKV // (3 * D_B))),
            pl.BlockSpec((nsb, rows, D_B), lambda b, c: (b, c, P_BZ // D_B)),
            pl.BlockSpec((nsb, rows, LANES), lambda b, c: (b, c, P_AB // LANES)),
            pl.BlockSpec((nsb, CONV_B - 1, 3 * D_B), lambda b, c: (b, 0, 0)),
            pl.BlockSpec((nsb, H_B, HEAD_DIM, HEAD_DIM), lambda b, c: (b, 0, 0, 0)),
            pl.BlockSpec((CONV_B, 3 * D_B), lambda b, c: (0, 0)),
            pl.BlockSpec((1, LANES), lambda b, c: (0, 0)),
            pl.BlockSpec((1, LANES), lambda b, c: (0, 0)),
            pl.BlockSpec((1, HEAD_DIM), lambda b, c: (0, 0)),
        ],
        out_specs=[
            pl.BlockSpec((nsb, rows, D_B), lambda b, c: (b, c, 0)),
            pl.BlockSpec((nsb, H_B, HEAD_DIM, HEAD_DIM), lambda b, c: (b, 0, 0, 0)),
            pl.BlockSpec((nsb, CONV_B - 1, 3 * D_B), lambda b, c: (b, 0, 0)),
        ],
        out_shape=[
            jax.ShapeDtypeStruct((nseq, m // nseq, D_B), out_dtype),
            jax.ShapeDtypeStruct((nseq, H_B, HEAD_DIM, HEAD_DIM), f32),
            jax.ShapeDtypeStruct((nseq, CONV_B - 1, 3 * D_B), f32),
        ],
        scratch_shapes=[pltpu.VMEM((nsb, rows + 8, 3 * D_B), f32)],
        compiler_params=_params("parallel", "arbitrary"),
        name="delta",
    )(p3, p3, p3, hist, s0, cw, alog, dtb, ng)
    return ob.reshape(m, D_B), s_new, conv_new


def _lam(lam_ref, lam_init):
    lv = lam_ref[...]
    a = jnp.sum(lv[0:1] * lv[1:2], axis=-1, keepdims=True)
    b = jnp.sum(lv[2:3] * lv[3:4], axis=-1, keepdims=True)
    return jnp.exp(a) - jnp.exp(b) + lam_init


def _attn_kernel(lam_ref, q_ref, k_ref, v_ref, g_ref, o_ref, qs, m_s, l_s, acc, *, t, lam_init):
    qi = pl.program_id(2)
    q = q_ref[...] * (DC_HALF ** -0.5)
    lane = lax.broadcasted_iota(jnp.int32, q.shape, 1)
    qs[0] = jnp.where(lane < DC_HALF, q, 0.0).astype(bf16)
    qs[1] = jnp.where(lane >= DC_HALF, q, 0.0).astype(bf16)
    m_s[...] = jnp.full(m_s.shape, NEG_BIG, f32)
    l_s[...] = jnp.zeros(l_s.shape, f32)
    acc[...] = jnp.zeros(acc.shape, f32)

    def block(j, diag):
        off = pl.multiple_of(j * t, t)
        k = k_ref[pl.ds(off, t), :].astype(bf16)
        v = v_ref[pl.ds(off, t), :].astype(bf16)
        cs = range(2)
        s = [lax.dot_general(qs[c], k, NT, preferred_element_type=f32) for c in cs]
        if diag:
            row = lax.broadcasted_iota(jnp.int32, (t, t), 0)
            col = lax.broadcasted_iota(jnp.int32, (t, t), 1)
            s = [jnp.where(col <= row, sc, NEG_BIG) for sc in s]
        m_old = [m_s[c] for c in cs]
        m_new = [jnp.maximum(m_old[c], jnp.max(s[c], axis=-1, keepdims=True)) for c in cs]
        alpha = [jnp.exp(m_old[c] - m_new[c]) for c in cs]
        p = [jnp.exp(s[c] - jnp.concatenate([m_new[c]] * (t // HEAD_DIM), axis=1)) for c in cs]
        pv = [jnp.dot(p[c].astype(bf16), v, preferred_element_type=f32) for c in cs]
        for c in cs:
            l_s[c] = alpha[c] * l_s[c] + jnp.sum(p[c], axis=-1, keepdims=True)
            acc[c] = alpha[c] * acc[c] + pv[c]
            m_s[c] = m_new[c]

    def body(j, carry):
        block(j, False)
        return carry

    lax.fori_loop(0, qi, body, 0)
    block(qi, True)

    o = acc[0] / l_s[0] - _lam(lam_ref, lam_init) * (acc[1] / l_s[1])
    o = _rms(o, g_ref[...]) * (1.0 - lam_init)
    o_ref[...] = o.astype(o_ref.dtype)


def _attn(p, kbuf, vbuf, layer, lamv, g, nseq, seq, t, lam_init):
    m = p.shape[0]
    nb = seq // t
    cq = P_CQ // HEAD_DIM
    kern = functools.partial(_attn_kernel, t=t, lam_init=lam_init)
    kv_spec = pl.BlockSpec((None, None, None, seq, HEAD_DIM), lambda b, h, i: (b, layer, h, 0, 0))
    return pl.pallas_call(
        kern,
        grid=(nseq, H_C, nb),
        in_specs=[
            pl.BlockSpec((4, DC_HALF), lambda b, h, i: (0, 0)),
            pl.BlockSpec((t, HEAD_DIM), lambda b, h, i: (b * nb + i, cq + h)),
            kv_spec,
            kv_spec,
            pl.BlockSpec((1, HEAD_DIM), lambda b, h, i: (0, 0)),
        ],
        out_specs=pl.BlockSpec((t, HEAD_DIM), lambda b, h, i: (b * nb + i, h)),
        out_shape=jax.ShapeDtypeStruct((m, D_C), bf16),
        scratch_shapes=[
            pltpu.VMEM((2, t, HEAD_DIM), bf16),
            pltpu.VMEM((2, t, HEAD_DIM), f32),
            pltpu.VMEM((2, t, HEAD_DIM), f32),
            pltpu.VMEM((2, t, HEAD_DIM), f32),
        ],
        compiler_params=_params("parallel", "parallel", "arbitrary"),
        name="attn",
    )(lamv, p, kbuf, vbuf, g)


def _paged_attn_kernel(pt_ref, lam_ref, q0_ref, q1_ref, q2_ref, kn_ref, vn_ref, g_ref, *rest, npg, lam_init):
    k_refs = rest[:npg]
    v_refs = rest[npg:2 * npg]
    o_ref = rest[2 * npg]
    qs, kb, vb, m_s, l_s, acc = rest[2 * npg + 1:]
    gi = pl.program_id(1)
    hq = 2 * SAMPLE_ROWS

    @pl.when(gi == 0)
    def _():
        q = jnp.concatenate([q0_ref[...], q1_ref[...], q2_ref[...]], axis=1) * (DC_HALF ** -0.5)
        lane = lax.broadcasted_iota(jnp.int32, (SAMPLE_ROWS, HEAD_DIM), 1)
        for h in range(H_C):
            qh = q[:, h * HEAD_DIM:(h + 1) * HEAD_DIM]
            qs[h] = jnp.concatenate([jnp.where(lane < DC_HALF, qh, 0.0),
                                     jnp.where(lane >= DC_HALF, qh, 0.0)], axis=0).astype(bf16)
        m_s[...] = jnp.full(m_s.shape, NEG_BIG, f32)
        l_s[...] = jnp.zeros(l_s.shape, f32)
        acc[...] = jnp.zeros(acc.shape, f32)

    def update(s, v_of_head):
        m_old = m_s[...]
        m_new = jnp.maximum(m_old, jnp.max(s, axis=-1, keepdims=True))
        alpha = jnp.exp(m_old - m_new)
        p = jnp.exp(s - jnp.concatenate([m_new] * (s.shape[1] // LANES), axis=1))
        l_s[...] = alpha * l_s[...] + jnp.sum(p, axis=-1, keepdims=True)
        p16 = p.astype(bf16)
        pv = jnp.concatenate(
            [jnp.dot(p16[h * hq:(h + 1) * hq], v_of_head(h), preferred_element_type=f32) for h in range(H_C)],
            axis=0)
        acc[...] = alpha * acc[...] + pv
        m_s[...] = m_new

    for i in range(npg):
        for h in range(H_C):
            kb[h, i * PAGE_SIZE:(i + 1) * PAGE_SIZE, :] = k_refs[i][h].astype(bf16)
            vb[h, i * PAGE_SIZE:(i + 1) * PAGE_SIZE, :] = v_refs[i][h].astype(bf16)
    s_all = jnp.concatenate(
        [lax.dot_general(qs[h], kb[h], NT, preferred_element_type=f32) for h in range(H_C)], axis=0)
    update(s_all, lambda h: vb[h])

    @pl.when(gi == pl.num_programs(1) - 1)
    def _():
        pad = jnp.zeros((LANES - SAMPLE_ROWS, HEAD_DIM), f32)
        head = lambda x_ref, h: jnp.concatenate([x_ref[h], pad], axis=0).astype(bf16)
        s = jnp.concatenate(
            [lax.dot_general(qs[h], head(kn_ref, h), NT, preferred_element_type=f32) for h in range(H_C)], axis=0)
        row = lax.broadcasted_iota(jnp.int32, s.shape, 0)
        col = lax.broadcasted_iota(jnp.int32, s.shape, 1)
        s = jnp.where(col <= jnp.bitwise_and(row, SAMPLE_ROWS - 1), s, NEG_BIG)
        update(s, lambda h: head(vn_ref, h))
        o12 = acc[...] / l_s[...]
        lam = _lam(lam_ref, lam_init)
        for h in range(H_C):
            r1 = h * hq
            r2 = r1 + SAMPLE_ROWS
            o = o12[r1:r1 + SAMPLE_ROWS] - lam * o12[r2:r2 + SAMPLE_ROWS]
            o_ref[:, h * HEAD_DIM:(h + 1) * HEAD_DIM] = _rms(o, g_ref[...]) * (1.0 - lam_init)


def _paged_attn(p, kbuf, vbuf, lamv, g, cache_k, cache_v, page_table, layer, npg, lam_init):
    m = p.shape[0]
    nseq = m // SAMPLE_ROWS
    n_pages = page_table.shape[0] // nseq
    ngroups = n_pages // npg
    nq = 2 * H_C * SAMPLE_ROWS

    def page_spec(i):
        return pl.BlockSpec((None, None, H_C, PAGE_SIZE, HEAD_DIM),
                            lambda b, gq, pt: (pt[b * n_pages + gq * npg + i], layer, 0, 0, 0))

    new_spec = pl.BlockSpec((None, None, H_C, SAMPLE_ROWS, HEAD_DIM), lambda b, gq, pt: (b, layer, 0, 0, 0))
    q_spec = lambda c: pl.BlockSpec((SAMPLE_ROWS, 2 * HEAD_DIM), lambda b, gq, pt: (b, P_CQ // (2 * HEAD_DIM) + c))
    kern = functools.partial(_paged_attn_kernel, npg=npg, lam_init=lam_init)
    grid_spec = pltpu.PrefetchScalarGridSpec(
        num_scalar_prefetch=1,
        grid=(nseq, ngroups),
        in_specs=[
            pl.BlockSpec((4, DC_HALF), lambda b, gq, pt: (0, 0)),
            q_spec(0), q_spec(1), q_spec(2), new_spec, new_spec,
            pl.BlockSpec((1, HEAD_DIM), lambda b, gq, pt: (0, 0)),
        ] + [page_spec(i) for i in range(npg)] + [page_spec(i) for i in range(npg)],
        out_specs=pl.BlockSpec((SAMPLE_ROWS, D_C), lambda b, gq, pt: (b, 0)),
        scratch_shapes=[
            pltpu.VMEM((H_C, 2 * SAMPLE_ROWS, HEAD_DIM), bf16),
            pltpu.VMEM((H_C, npg * PAGE_SIZE, HEAD_DIM), bf16),
            pltpu.VMEM((H_C, npg * PAGE_SIZE, HEAD_DIM), bf16),
            pltpu.VMEM((nq, LANES), f32),
            pltpu.VMEM((nq, LANES), f32),
            pltpu.VMEM((nq, HEAD_DIM), f32),
        ],
    )
    return pl.pallas_call(
        kern,
        grid_spec=grid_spec,
        out_shape=jax.ShapeDtypeStruct((m, D_C), f32),
        compiler_params=_params("parallel", "arbitrary"),
        name="paged_attn",
    )(page_table, lamv, p, p, p, kbuf, vbuf, g, *([cache_k] * npg), *([cache_v] * npg))


def _out_proj_kernel(a_ref, b_ref, c_ref, x_ref, w_ref, g_ref, xo_ref, h_ref):
    mix = jnp.concatenate([a_ref[...].astype(bf16), b_ref[...].astype(bf16), c_ref[...].astype(bf16)], axis=1)
    for n in range(D_MODEL // MXU_WIDTH):
        cols = slice(n * MXU_WIDTH, (n + 1) * MXU_WIDTH)
        xo_ref[:, cols] = x_ref[:, cols] + jnp.dot(mix, w_ref[:, cols], preferred_element_type=f32)
    h_ref[...] = _rms(xo_ref[...], g_ref[...]).astype(bf16)


def _out_proj(a, b, c, x, wo, layer, g, tm):
    m = x.shape[0]
    row = lambda w: pl.BlockSpec((tm, w), lambda i: (i, 0))
    return pl.pallas_call(
        _out_proj_kernel,
        grid=(m // tm,),
        in_specs=[
            row(D_A), row(D_B), row(D_C), row(D_MODEL),
            pl.BlockSpec((None, D_MODEL, D_MODEL), lambda i: (layer, 0, 0), pipeline_mode=pl.Buffered(1)),
            pl.BlockSpec((1, D_MODEL), lambda i: (0, 0)),
        ],
        out_specs=[row(D_MODEL), row(D_MODEL)],
        out_shape=[jax.ShapeDtypeStruct((m, D_MODEL), f32), jax.ShapeDtypeStruct((m, D_MODEL), bf16)],
        compiler_params=_params("parallel"),
        name="out_proj",
    )(a, b, c, x, wo, g)


def _ffn_up_kernel(h_ref, wg_ref, wv_ref, cwg_ref, cwv_ref, cbg_ref, cbv_ref, hg_ref, hv_ref,
                   act_ref, ng_ref, nv_ref, *, nseq, seq_rows, t_valid):
    h = h_ref[...]
    tf = wg_ref.shape[1]
    ts = min(tf, FFN_SUB)
    rin = lax.broadcasted_iota(jnp.int32, (nseq, seq_rows, ts), 1)

    def conv(w_ref, cw_ref, cb_ref, hist_ref, new_ref, cols):
        w = w_ref[:, cols].astype(bf16)
        rb = min(h.shape[0], FFN_ROWS)
        up = jnp.concatenate([jnp.dot(h[r:r + rb], w, preferred_element_type=f32)
                              for r in range(0, h.shape[0], rb)], axis=0)
        up3 = up.reshape(nseq, seq_rows, ts)
        r1 = pltpu.roll(up, 1, axis=0).reshape(nseq, seq_rows, ts)
        r2 = pltpu.roll(up, 2, axis=0).reshape(nseq, seq_rows, ts)
        hist = hist_ref[:, :, cols]
        h0 = hist[:, 0:1, :]
        h1 = hist[:, 1:2, :]
        prev1 = jnp.where(rin == 0, h1, r1)
        prev2 = jnp.where(rin == 0, h0, jnp.where(rin == 1, h1, r2))
        cw = cw_ref[:, cols]
        y = prev2 * cw[0:1] + prev1 * cw[1:2] + up3 * cw[2:3] + cb_ref[:, cols]
        new_ref[:, :, cols] = up3[:, t_valid - 2:t_valid, :]
        return y

    for sub in range(tf // ts):
        cols = slice(sub * ts, (sub + 1) * ts)
        yg = conv(wg_ref, cwg_ref, cbg_ref, hg_ref, ng_ref, cols)
        yv = conv(wv_ref, cwv_ref, cbv_ref, hv_ref, nv_ref, cols)
        act = yg * _sigmoid(yg) * yv
        act_ref[:, cols] = act.reshape(nseq * seq_rows, ts).astype(bf16)


def _ffn_up(h2, w_up, layer, cw, cb, hist, nseq_blk, seq_rows, t_valid, tf):
    m = h2.shape[0]
    tm = nseq_blk * seq_rows
    nseq = m // seq_rows
    nj = D_FF // tf
    kern = functools.partial(_ffn_up_kernel, nseq=nseq_blk, seq_rows=seq_rows, t_valid=t_valid)
    gate = lambda i, j: (0, j)
    val = lambda i, j: (0, nj + j)
    return pl.pallas_call(
        kern,
        grid=(m // tm, nj),
        in_specs=[
            pl.BlockSpec((tm, D_MODEL), lambda i, j: (i, 0)),
            pl.BlockSpec((None, D_MODEL, tf), lambda i, j: (layer, 0, j)),
            pl.BlockSpec((None, D_MODEL, tf), lambda i, j: (layer, 0, nj + j)),
            pl.BlockSpec((CONV_FFN, tf), gate),
            pl.BlockSpec((CONV_FFN, tf), val),
            pl.BlockSpec((1, tf), gate),
            pl.BlockSpec((1, tf), val),
            pl.BlockSpec((nseq_blk, CONV_FFN - 1, tf), lambda i, j: (i, 0, j)),
            pl.BlockSpec((nseq_blk, CONV_FFN - 1, tf), lambda i, j: (i, 0, nj + j)),
        ],
        out_specs=[
            pl.BlockSpec((tm, tf), lambda i, j: (i, j)),
            pl.BlockSpec((nseq_blk, CONV_FFN - 1, tf), lambda i, j: (i, 0, j)),
            pl.BlockSpec((nseq_blk, CONV_FFN - 1, tf), lambda i, j: (i, 0, j)),
        ],
        out_shape=[
            jax.ShapeDtypeStruct((m, D_FF), bf16),
            jax.ShapeDtypeStruct((nseq, CONV_FFN - 1, D_FF), f32),
            jax.ShapeDtypeStruct((nseq, CONV_FFN - 1, D_FF), f32),
        ],
        compiler_params=_params("parallel", "arbitrary"),
        name="ffn_up",
    )(h2, w_up, w_up, cw, cw, cb, cb, hist, hist)


def _ffn_down_kernel(a_ref, w_ref, x_ref, g_ref, o_ref, *, final_norm):
    a = a_ref[...]
    for n in range(D_MODEL // MXU_WIDTH):
        cols = slice(n * MXU_WIDTH, (n + 1) * MXU_WIDTH)
        o_ref[:, cols] = x_ref[:, cols] + jnp.dot(a, w_ref[:, cols], preferred_element_type=f32)
    if final_norm:
        o_ref[...] = _rms(o_ref[...], g_ref[...])


def _ffn_down(act, w_down, layer, x, g, tm, final_norm):
    m = x.shape[0]
    kern = functools.partial(_ffn_down_kernel, final_norm=final_norm)
    return pl.pallas_call(
        kern,
        grid=(m // tm,),
        in_specs=[
            pl.BlockSpec((tm, D_FF), lambda i: (i, 0)),
            pl.BlockSpec((None, D_FF, D_MODEL), lambda i: (layer, 0, 0), pipeline_mode=pl.Buffered(1)),
            pl.BlockSpec((tm, D_MODEL), lambda i: (i, 0)),
            pl.BlockSpec((1, D_MODEL), lambda i: (0, 0)),
        ],
        out_specs=pl.BlockSpec((tm, D_MODEL), lambda i: (i, 0)),
        out_shape=jax.ShapeDtypeStruct((m, D_MODEL), f32),
        compiler_params=_params("parallel"),
        name="ffn_down",
    )(act, w_down, x, g)


def _pad_lanes(v, offset=0):
    return jnp.zeros((1, LANES), f32).at[0, offset:offset + v.shape[0]].set(v)


def _prep_weights(w):
    wt = jnp.swapaxes(w["w_in"], 1, 2).astype(bf16)
    stacked = dict(
        w_in_t=wt,
        w_q=jnp.concatenate([wt[:, OFF_CQ:OFF_CK], wt[:, OFF_BA:OFF_CQ],
                             jnp.zeros((DEPTH, N_MAIN - P_AB - 2 * H_B, D_MODEL), bf16)], axis=1),
        w_kv=wt[:, OFF_CK:N_IN],
        w_out=w["w_out"].astype(bf16), w_up=w["w_up"], w_down=w["w_down"].astype(bf16))
    out = []
    for l in range(DEPTH):
        out.append(dict(
            norm1_g=w["norm1_g"][l][None],
            a_norm_g=w["a_norm_g"][l].reshape(1, D_A),
            a_ws=w["a_ws"][l], a_bs=w["a_bs"][l],
            b_conv_w=w["b_conv_w"][l],
            alog=_pad_lanes(w["b_a_log"][l]), dtb=_pad_lanes(w["b_dt_bias"][l]),
            b_norm_g=w["b_norm_g"][l][None],
            lamv=jnp.stack([w["c_lam_q1"][l], w["c_lam_k1"][l], w["c_lam_q2"][l], w["c_lam_k2"][l]]),
            c_norm_g=w["c_norm_g"][l][None],
            norm2_g=w["norm2_g"][l][None],
            ffn_conv_w=w["ffn_conv_w"][l], ffn_conv_b=w["ffn_conv_b"][l][None],
        ))
    return out, stacked


def _blocks(m):
    return dict(
        in_rows=min(2 * ACC_ROWS, m),
        rows=min(ACC_ROWS, m),
        ffn_cols=2 * MXU_WIDTH,
        attn=ACC_ROWS,
    )


def _run_trunk(x, lw, sw, final_g, nseq, seq_rows, t_valid, past):
    m = x.shape[0]
    sample = past is not None
    blk = _blocks(m)
    tm = blk["rows"]
    kbuf = jnp.zeros((nseq, DEPTH, H_C, seq_rows, HEAD_DIM), f32)
    vbuf = jnp.zeros((nseq, DEPTH, H_C, seq_rows, HEAD_DIM), f32)
    deltas, convs, ffns, chunk_vs = [], [], [], []
    for l in range(DEPTH):
        w = lw[l]
        lam_init = 0.8 - 0.6 * math.exp(-0.3 * l)
        p, h = _in_proj(x, w["norm1_g"], sw["w_in_t"], sw["w_q"], l, blk["in_rows"])
        kbuf, vbuf = _kv_proj(h, sw["w_kv"], kbuf, vbuf, l, seq_rows, tm)
        if sample:
            a_w = jnp.tile(w["a_ws"][:, :seq_rows, :seq_rows], (1, nseq, nseq))
            a_b = jnp.tile(w["a_bs"][:, :seq_rows].T, (nseq, 1))
            oa, va = _chunk_mlp(p, w["a_norm_g"], a_w, jnp.pad(a_b, ((0, 0), (0, LANES - H_A))),
                                m, seq_rows, m, f32)
            ob, s_new, conv_new = _delta(
                p, past["state_conv_qkv"][:, l], past["state_delta"][:, l], w["b_conv_w"], w["alog"],
                w["dtb"], w["b_norm_g"], nseq, seq_rows, t_valid, 1, f32)
            oc = _paged_attn(p, kbuf, vbuf, w["lamv"], w["c_norm_g"], past["cache_k"], past["cache_v"],
                             past["page_table"], l, PAGES_PER_STEP, lam_init)
            ffn_hist = past["state_ffn_conv"][:, l]
            nseq_blk = nseq
        else:
            a_b = jnp.pad(w["a_bs"].T, ((0, 0), (0, LANES - H_A)))
            oa, va = _chunk_mlp(p, w["a_norm_g"], w["a_ws"], a_b, CHUNK_A, CHUNK_A, tm, bf16)
            ob, s_new, conv_new = _delta(
                p, jnp.zeros((nseq, CONV_B - 1, 3 * D_B), f32),
                jnp.zeros((nseq, H_B, HEAD_DIM, HEAD_DIM), f32), w["b_conv_w"], w["alog"], w["dtb"],
                w["b_norm_g"], nseq, CHUNK_B, CHUNK_B, seq_rows // CHUNK_B, bf16)
            oc = _attn(p, kbuf, vbuf, l, w["lamv"], w["c_norm_g"], nseq, seq_rows,
                       min(blk["attn"], seq_rows), lam_init)
            ffn_hist = jnp.zeros((nseq, CONV_FFN - 1, 2 * D_FF), f32)
            nseq_blk = 1
        x, h2 = _out_proj(oa, ob, oc, x, sw["w_out"], l, w["norm2_g"], tm)
        act, hg, hv = _ffn_up(h2, sw["w_up"], l, w["ffn_conv_w"], w["ffn_conv_b"], ffn_hist,
                              nseq_blk, seq_rows, t_valid, blk["ffn_cols"])
        x = _ffn_down(act, sw["w_down"], l, x, final_g[None], tm, l == DEPTH - 1)
        deltas.append(s_new)
        convs.append(conv_new)
        ffns.append(jnp.concatenate([hg, hv], axis=-1))
        if sample:
            chunk_vs.append(va.reshape(nseq, seq_rows, H_A, HEAD_DIM)[:, :t_valid])
    y = x.reshape(nseq, seq_rows, D_MODEL)[:, :t_valid]
    k_rows = jnp.transpose(kbuf, (0, 1, 3, 2, 4))[:, :, :t_valid]
    v_rows = jnp.transpose(vbuf, (0, 1, 3, 2, 4))[:, :, :t_valid]
    stack = lambda xs: jnp.stack(xs, axis=1)
    deltas = jnp.transpose(jnp.stack(deltas, axis=0), (1, 0, 2, 3, 4))
    convs = jnp.transpose(jnp.stack(convs, axis=2), (0, 2, 1, 3))
    return (y, k_rows, v_rows, deltas, convs, stack(ffns), stack(chunk_vs) if sample else None)


def kernel(x_prompt, x_sample, cache_k, cache_v, page_table, state_delta, state_conv_qkv, state_ffn_conv,
           norm1_g, w_in, a_norm_g, a_ws, a_bs, b_conv_w, b_a_log, b_dt_bias, b_norm_g, c_lam_q1, c_lam_k1,
           c_lam_q2, c_lam_k2, c_norm_g, w_out, norm2_g, w_up, ffn_conv_w, ffn_conv_b, w_down, final_g):
    weights = dict(norm1_g=norm1_g, w_in=w_in, a_norm_g=a_norm_g, a_ws=a_ws, a_bs=a_bs, b_conv_w=b_conv_w,
                   b_a_log=b_a_log, b_dt_bias=b_dt_bias, b_norm_g=b_norm_g, c_lam_q1=c_lam_q1,
                   c_lam_k1=c_lam_k1, c_lam_q2=c_lam_q2, c_lam_k2=c_lam_k2, c_norm_g=c_norm_g, w_out=w_out,
                   norm2_g=norm2_g, w_up=w_up, ffn_conv_w=ffn_conv_w, ffn_conv_b=ffn_conv_b, w_down=w_down)
    lw, sw = _prep_weights(weights)

    batch, seq, _ = x_prompt.shape
    yp, pk, pv, pd, pc, pf, _ = _run_trunk(
        x_prompt.reshape(batch * seq, D_MODEL), lw, sw, final_g, batch, seq, seq, None)

    dbatch, dseq, _ = x_sample.shape
    xs = jnp.pad(x_sample, ((0, 0), (0, SAMPLE_ROWS - dseq), (0, 0))).reshape(dbatch * SAMPLE_ROWS, D_MODEL)
    past = dict(
        cache_k=jnp.transpose(cache_k, (0, 1, 3, 2, 4)),
        cache_v=jnp.transpose(cache_v, (0, 1, 3, 2, 4)),
        page_table=page_table.reshape(-1).astype(jnp.int32),
        state_delta=state_delta, state_conv_qkv=state_conv_qkv, state_ffn_conv=state_ffn_conv)
    ys, sk, sv, sd, sc, sf, scv = _run_trunk(xs, lw, sw, final_g, dbatch, SAMPLE_ROWS, dseq, past)
    return (yp, ys, pk, pv, pd, pc, pf, sk, sv, sd, sc, sf, scv)
```
